```python
import math
import jax
import jax.numpy as jnp
from jax import lax
import numpy as np

D_MODEL = 1024
BATCH = 8
SEQ = 2048
DEPTH = 2
DEC_BATCH = 32
DEC_SEQ = 1
PAST_LEN = 16384
PAGE_SIZE = 128

N_HEADS = 16
HEAD_DIM = D_MODEL // N_HEADS
MOBA_BLOCK = 256
MOBA_TOPK = 3
Q_BLOCK = 128
REL_BUCKETS = 32
REL_MAX_DIST = 4096
M_HEADS = 4
M_QK_DIM = D_MODEL // (2 * M_HEADS)
M_V_DIM = D_MODEL // M_HEADS
MLSTM_CHUNK = 64
D_FF = 2816
CONV_WIDTH = 3
N_ATTN_LAYERS = (DEPTH + 1) // 2
N_MLSTM_LAYERS = DEPTH // 2
EPS = 1e-6
NEG_INF = -1e30
F32 = jnp.float32

kernel_name = 'moba_mlstm_convffn_hybrid_step'


def rmsnorm(x, g):
    x32 = x.astype(F32)
    y = x32 * lax.rsqrt(jnp.mean(x32 * x32, axis=-1, keepdims=True) + EPS)
    return (y * g.astype(F32)).astype(x.dtype)


def t5_bucket(dist):
    n = jnp.maximum(dist, 0)
    max_exact = REL_BUCKETS // 2
    nf = jnp.maximum(n, 1).astype(F32)
    large = max_exact + (jnp.log(nf / max_exact) / math.log(REL_MAX_DIST / max_exact)
                         * (REL_BUCKETS - max_exact)).astype(jnp.int32)
    large = jnp.minimum(large, REL_BUCKETS - 1)
    return jnp.where(n < max_exact, n, large)


def qkv_heads(x, norm_g, w):
    n, t, _ = x.shape
    y = rmsnorm(x, norm_g) @ w
    q, k, v = jnp.split(y, 3, axis=-1)
    shp = (n, t, N_HEADS, HEAD_DIM)
    return q.reshape(shp), k.reshape(shp), v.reshape(shp)


def moba_prompt(q, k, v, bias_hk):
    b_, s, h, dh = q.shape
    nb = -(-s // MOBA_BLOCK)
    pad = nb * MOBA_BLOCK - s
    kb = jnp.pad(k, ((0, 0), (0, pad), (0, 0), (0, 0))).reshape(b_, nb, MOBA_BLOCK, h, dh)
    vb = jnp.pad(v, ((0, 0), (0, pad), (0, 0), (0, 0))).reshape(b_, nb, MOBA_BLOCK, h, dh)
    kmean = jnp.mean(kb.astype(F32), axis=2)
    n_cand = max(nb, MOBA_TOPK)
    nqb = s // Q_BLOCK
    qb = q.reshape(b_, nqb, Q_BLOCK, h, dh)
    scale = dh ** -0.5
    hidx = jnp.arange(h)[:, None, None]
    cand = jnp.arange(n_cand)
    offs = jnp.arange(MOBA_BLOCK)
    n_sel = MOBA_TOPK * MOBA_BLOCK

    def one_block(idx):
        bi = idx // nqb
        qi = idx % nqb
        qblk = qb[bi, qi].astype(F32)
        q0 = qi * Q_BLOCK
        ob = q0 // MOBA_BLOCK
        qpos = q0 + jnp.arange(Q_BLOCK)
        sc = jnp.einsum('qhd,nhd->hqn', qblk, kmean[bi])
        sc = jnp.pad(sc, ((0, 0), (0, 0), (0, n_cand - nb)), constant_values=-jnp.inf)
        sc = jnp.where(cand < ob, sc, -jnp.inf)
        _, sel = lax.top_k(sc, MOBA_TOPK)
        valid = jnp.repeat(sel < ob, MOBA_BLOCK, axis=-1)
        selc = jnp.minimum(sel, nb - 1)
        ksel = kb[bi, selc, :, hidx].reshape(h, Q_BLOCK, n_sel, dh).astype(F32)
        vsel = vb[bi, selc, :, hidx].reshape(h, Q_BLOCK, n_sel, dh).astype(F32)
        kpos = (selc[..., None] * MOBA_BLOCK + offs).reshape(h, Q_BLOCK, n_sel)
        lg_sel = (jnp.einsum('qhd,hqkd->hqk', qblk, ksel) * scale
                  + bias_hk[hidx, t5_bucket(qpos[None, :, None] - kpos)])
        lg_sel = jnp.where(valid, lg_sel, NEG_INF)
        kown = kb[bi, ob].astype(F32)
        vown = vb[bi, ob].astype(F32)
        kpos_own = ob * MOBA_BLOCK + offs
        lg_own = (jnp.einsum('qhd,khd->hqk', qblk, kown) * scale
                  + bias_hk[:, t5_bucket(qpos[:, None] - kpos_own[None, :])])
        lg_own = jnp.where(kpos_own[None, :] <= qpos[:, None], lg_own, NEG_INF)
        p = jax.nn.softmax(jnp.concatenate([lg_sel, lg_own], axis=-1), axis=-1)
        out = (jnp.einsum('hqk,hqkd->qhd', p[..., :n_sel], vsel)
               + jnp.einsum('hqk,khd->qhd', p[..., n_sel:], vown))
        return out.astype(q.dtype)

    out = lax.map(one_block, jnp.arange(b_ * nqb))
    return out.reshape(b_, s, h, dh)


def moba_sample(q, k, v, ck, cv, page_table, bias_hk):
    db, t, h, dh = q.shape
    n_pages = page_table.shape[1]
    past = n_pages * PAGE_SIZE
    ppb = MOBA_BLOCK // PAGE_SIZE
    nbp = past // MOBA_BLOCK
    n_own = (past % MOBA_BLOCK) // PAGE_SIZE
    scale = dh ** -0.5
    qf = q.astype(F32)
    qpos = past + jnp.arange(t)
    hidx = jnp.arange(h)[None, :, None, None, None]
    segs = []
    if nbp > 0:
        blk_pages = page_table[:, : nbp * ppb].reshape(db, nbp, ppb)
        page_mean = jnp.mean(ck.astype(F32), axis=1)
        kmean = jnp.mean(page_mean[blk_pages], axis=2)
        n_cand = max(nbp, MOBA_TOPK)
        sc = jnp.einsum('bthd,bnhd->bhtn', qf, kmean)
        sc = jnp.pad(sc, ((0, 0), (0, 0), (0, 0), (0, n_cand - nbp)), constant_values=-jnp.inf)
        _, sel = lax.top_k(sc, MOBA_TOPK)
        valid = jnp.repeat(sel < nbp, MOBA_BLOCK, axis=-1)
        selc = jnp.minimum(sel, nbp - 1)
        phys = blk_pages[jnp.arange(db)[:, None, None, None], selc]
        n_sel = MOBA_TOPK * MOBA_BLOCK
        ksel = ck[phys, :, hidx].reshape(db, h, t, n_sel, dh).astype(F32)
        vsel = cv[phys, :, hidx].reshape(db, h, t, n_sel, dh).astype(F32)
        kpos = (selc[..., None] * MOBA_BLOCK + jnp.arange(MOBA_BLOCK)).reshape(db, h, t, n_sel)
        lg = (jnp.einsum('bthd,bhtkd->bhtk', qf, ksel) * scale
              + bias_hk[hidx[..., 0], t5_bucket(qpos[None, None, :, None] - kpos)])
        lg = jnp.where(valid, lg, NEG_INF)
        segs.append((lg, vsel, 'bhtk,bhtkd->bthd'))
    if n_own > 0:
        own_pages = page_table[:, nbp * ppb: nbp * ppb + n_own]
        kown = ck[own_pages].reshape(db, n_own * PAGE_SIZE, h, dh).astype(F32)
        vown = cv[own_pages].reshape(db, n_own * PAGE_SIZE, h, dh).astype(F32)
        kpos = nbp * MOBA_BLOCK + jnp.arange(n_own * PAGE_SIZE)
        lg = (jnp.einsum('bthd,bkhd->bhtk', qf, kown) * scale
              + bias_hk[:, t5_bucket(qpos[:, None] - kpos[None, :])])
        segs.append((lg, vown, 'bhtk,bkhd->bthd'))
    tri = jnp.arange(t)[None, :] <= jnp.arange(t)[:, None]
    lg = (jnp.einsum('bthd,bshd->bhts', qf, k.astype(F32)) * scale
          + bias_hk[:, t5_bucket(qpos[:, None] - qpos[None, :])])
    lg = jnp.where(tri, lg, NEG_INF)
    segs.append((lg, v.astype(F32), 'bhts,bshd->bthd'))
    p = jax.nn.softmax(jnp.concatenate([sg[0] for sg in segs], axis=-1), axis=-1)
    out = jnp.zeros((db, t, h, dh), F32)
    off = 0
    for lg_i, val, eq in segs:
        w = lg_i.shape[-1]
        out = out + jnp.einsum(eq, p[..., off:off + w], val)
        off += w
    return out.astype(q.dtype)


def mlstm_chunkwise(q, k, v, ig, lf, c0, n0, m0):
    b_, h, t, _ = q.shape
    L = MLSTM_CHUNK if t % MLSTM_CHUNK == 0 else t
    nc = t // L

    def chunks(a):
        return jnp.moveaxis(a.reshape(b_, h, nc, L, *a.shape[3:]), 2, 0)

    causal = jnp.tril(jnp.ones((L, L), dtype=bool))

    def step(carry, inp):
        c, n, m = carry
        qc, kc, vc, ic, fc = inp
        bcum = jnp.cumsum(fc, axis=-1)
        logd = jnp.where(causal, bcum[..., :, None] - bcum[..., None, :] + ic[..., None, :], -jnp.inf)
        g = bcum + m[..., None]
        mt = jnp.maximum(g, jnp.max(logd, axis=-1))
        dmat = jnp.exp(logd - mt[..., None])
        inter = jnp.exp(g - mt)
        sm = jnp.einsum('bhtd,bhsd->bhts', qc, kc) * dmat
        num = jnp.einsum('bhts,bhsv->bhtv', sm, vc) + inter[..., None] * jnp.einsum('bhtd,bhdv->bhtv', qc, c)
        den = jnp.sum(sm, axis=-1) + inter * jnp.einsum('bhtd,bhd->bht', qc, n)
        hout = num / jnp.maximum(jnp.abs(den), jnp.exp(-mt))[..., None]
        m_new = mt[..., -1]
        w = jnp.exp(bcum[..., -1:] - bcum + ic - m_new[..., None])
        decay = jnp.exp(bcum[..., -1] + m - m_new)
        c_new = decay[..., None, None] * c + jnp.einsum('bhs,bhsd,bhsv->bhdv', w, kc, vc)
        n_new = decay[..., None] * n + jnp.einsum('bhs,bhsd->bhd', w, kc)
        return (c_new, n_new, m_new), hout

    (c, n, m), hs = lax.scan(step, (c0, n0, m0),
                             (chunks(q), chunks(k), chunks(v), chunks(ig), chunks(lf)))
    hout = jnp.moveaxis(hs, 0, 2).reshape(b_, h, t, -1)
    return hout, c, n, m


def mlstm_mixer(xn, w_in, b_gate, head_gain, w_out, c0, n0, m0):
    b_, t, _ = xn.shape
    hk = M_HEADS * M_QK_DIM
    hv = M_HEADS * M_V_DIM
    proj = (xn @ w_in).astype(F32)

    def heads(a, d):
        return a.reshape(b_, t, M_HEADS, d).transpose(0, 2, 1, 3)

    q = heads(proj[..., :hk], M_QK_DIM) * (M_QK_DIM ** -0.5)
    k = heads(proj[..., hk:2 * hk], M_QK_DIM)
    v = heads(proj[..., 2 * hk:2 * hk + hv], M_V_DIM)
    o = proj[..., 2 * hk + hv:2 * hk + 2 * hv]
    gates = proj[..., 2 * hk + 2 * hv:] + b_gate.astype(F32)
    ig = gates[..., :M_HEADS].transpose(0, 2, 1)
    lf = jax.nn.log_sigmoid(gates[..., M_HEADS:]).transpose(0, 2, 1)
    hcell, c, n, m = mlstm_chunkwise(q, k, v, ig, lf, c0.astype(F32), n0.astype(F32), m0.astype(F32))
    hcell = hcell.transpose(0, 2, 1, 3)
    hcell = hcell * lax.rsqrt(jnp.mean(hcell * hcell, axis=-1, keepdims=True) + EPS)
    hcell = hcell.reshape(b_, t, hv) * head_gain.astype(F32) * jax.nn.sigmoid(o)
    return hcell.astype(xn.dtype) @ w_out, c, n, m


def conv_ffn(xn, w_up, conv_w, conv_b, w_down, buf):
    t = xn.shape[1]
    up = xn @ w_up
    ug, uv = up[..., :D_FF], up[..., D_FF:]
    ext = jnp.concatenate([buf.astype(ug.dtype), ug], axis=1)
    conv = conv_b + conv_w[0] * ext[:, 0:t]
    for j in range(1, CONV_WIDTH):
        conv = conv + conv_w[j] * ext[:, j:j + t]
    y = (jax.nn.silu(conv) * uv) @ w_down
    return y, ext[:, -(CONV_WIDTH - 1):]


def setup_inputs(seed: int = 0) -> dict:
    key = jax.random.key(seed)
    ks = jax.random.split(key, 26)
    n_pages = PAST_LEN // PAGE_SIZE
    n_used = DEC_BATCH * n_pages
    n_phys = n_used + (n_used + 3) // 4
    hk = M_HEADS * M_QK_DIM
    hv = M_HEADS * M_V_DIM
    m_in = 2 * hk + 2 * hv + 2 * M_HEADS

    def nrm(k, shape, s=1.0):
        return jax.random.normal(k, shape, F32) * s

    x_prompt = nrm(ks[0], (BATCH, SEQ, D_MODEL))
    x_sample = nrm(ks[1], (DEC_BATCH, DEC_SEQ, D_MODEL))
    cache_k = nrm(ks[2], (N_ATTN_LAYERS, n_phys, PAGE_SIZE, N_HEADS, HEAD_DIM))
    cache_v = nrm(ks[3], (N_ATTN_LAYERS, n_phys, PAGE_SIZE, N_HEADS, HEAD_DIM))
    state_C = nrm(ks[4], (N_MLSTM_LAYERS, DEC_BATCH, M_HEADS, M_QK_DIM, M_V_DIM), 0.1)
    state_n = nrm(ks[5], (N_MLSTM_LAYERS, DEC_BATCH, M_HEADS, M_QK_DIM), 0.1)
    state_m = nrm(ks[6], (N_MLSTM_LAYERS, DEC_BATCH, M_HEADS))
    state_conv = nrm(ks[7], (DEPTH, DEC_BATCH, CONV_WIDTH - 1, D_FF))
    page_table = jax.random.permutation(ks[8], n_phys)[:n_used].reshape(DEC_BATCH, n_pages).astype(jnp.int32)
    rel_bias = nrm(ks[9], (REL_BUCKETS, N_HEADS), 0.5)
    attn_norm = 1.0 + nrm(ks[10], (N_ATTN_LAYERS, D_MODEL), 0.1)
    w_qkv = nrm(ks[11], (N_ATTN_LAYERS, D_MODEL, 3 * N_HEADS * HEAD_DIM), D_MODEL ** -0.5)
    w_attn_out = nrm(ks[12], (N_ATTN_LAYERS, N_HEADS * HEAD_DIM, D_MODEL), (N_HEADS * HEAD_DIM) ** -0.5)
    mlstm_norm = 1.0 + nrm(ks[13], (N_MLSTM_LAYERS, D_MODEL), 0.1)
    w_mlstm_in = nrm(ks[14], (N_MLSTM_LAYERS, D_MODEL, m_in), D_MODEL ** -0.5)
    b_i = nrm(ks[15], (N_MLSTM_LAYERS, M_HEADS), 0.1)
    b_f = jnp.linspace(3.0, 6.0, M_HEADS, dtype=F32)[None, :] + nrm(ks[16], (N_MLSTM_LAYERS, M_HEADS), 0.1)
    b_mlstm_gate = jnp.concatenate([b_i, b_f], axis=-1)
    mlstm_head_gain = 1.0 + nrm(ks[17], (N_MLSTM_LAYERS, hv), 0.1)
    w_mlstm_out = nrm(ks[18], (N_MLSTM_LAYERS, hv, D_MODEL), hv ** -0.5)
    ffn_norm = 1.0 + nrm(ks[19], (DEPTH, D_MODEL), 0.1)
    w_ffn_up = nrm(ks[20], (DEPTH, D_MODEL, 2 * D_FF), D_MODEL ** -0.5)
    ffn_conv_w = nrm(ks[21], (DEPTH, CONV_WIDTH, D_FF), CONV_WIDTH ** -0.5)
    ffn_conv_b = nrm(ks[22], (DEPTH, D_FF), 0.02)
    w_ffn_down = nrm(ks[23], (DEPTH, D_FF, D_MODEL), D_FF ** -0.5)
    final_norm = 1.0 + nrm(ks[24], (D_MODEL,), 0.1)
    return {'x_prompt': x_prompt, 'x_sample': x_sample, 'cache_k': cache_k, 'cache_v': cache_v,
            'state_C': state_C, 'state_n': state_n, 'state_m': state_m, 'state_conv': state_conv,
            'page_table': page_table, 'rel_bias': rel_bias, 'attn_norm': attn_norm, 'w_qkv': w_qkv,
            'w_attn_out': w_attn_out, 'mlstm_norm': mlstm_norm, 'w_mlstm_in': w_mlstm_in,
            'b_mlstm_gate': b_mlstm_gate, 'mlstm_head_gain': mlstm_head_gain, 'w_mlstm_out': w_mlstm_out,
            'ffn_norm': ffn_norm, 'w_ffn_up': w_ffn_up, 'ffn_conv_w': ffn_conv_w, 'ffn_conv_b': ffn_conv_b,
            'w_ffn_down': w_ffn_down, 'final_norm': final_norm}


def reference(x_prompt, x_sample, cache_k, cache_v, state_C, state_n, state_m, state_conv, page_table,
              rel_bias, attn_norm, w_qkv, w_attn_out, mlstm_norm, w_mlstm_in, b_mlstm_gate,
              mlstm_head_gain, w_mlstm_out, ffn_norm, w_ffn_up, ffn_conv_w, ffn_conv_b, w_ffn_down,
              final_norm):
    bias_hk = rel_bias.T.astype(F32)
    hp, hs = x_prompt, x_sample
    bp, sp, d = hp.shape
    bs, ts, _ = hs.shape
    kp_l, vp_l, ks_l, vs_l = [], [], [], []
    cp_l, np_l, mp_l, cs_l, ns_l, ms_l = [], [], [], [], [], []
    convp_l, convs_l = [], []
    for i in range(DEPTH):
        li = i // 2
        if i % 2 == 0:
            q, k, v = qkv_heads(hp, attn_norm[li], w_qkv[li])
            hp = hp + moba_prompt(q, k, v, bias_hk).reshape(bp, sp, d) @ w_attn_out[li]
            kp_l.append(k)
            vp_l.append(v)
            q, k, v = qkv_heads(hs, attn_norm[li], w_qkv[li])
            hs = hs + moba_sample(q, k, v, cache_k[li], cache_v[li], page_table,
                                  bias_hk).reshape(bs, ts, d) @ w_attn_out[li]
            ks_l.append(k.astype(cache_k.dtype))
            vs_l.append(v.astype(cache_v.dtype))
        else:
            zc = jnp.zeros((bp, M_HEADS, M_QK_DIM, M_V_DIM), F32)
            zn = jnp.zeros((bp, M_HEADS, M_QK_DIM), F32)
            zm = jnp.zeros((bp, M_HEADS), F32)
            y, c, n, m = mlstm_mixer(rmsnorm(hp, mlstm_norm[li]), w_mlstm_in[li], b_mlstm_gate[li],
                                     mlstm_head_gain[li], w_mlstm_out[li], zc, zn, zm)
            hp = hp + y
            cp_l.append(c.astype(state_C.dtype))
            np_l.append(n.astype(state_n.dtype))
            mp_l.append(m.astype(state_m.dtype))
            y, c, n, m = mlstm_mixer(rmsnorm(hs, mlstm_norm[li]), w_mlstm_in[li], b_mlstm_gate[li],
                                     mlstm_head_gain[li], w_mlstm_out[li], state_C[li], state_n[li], state_m[li])
            hs = hs + y
            cs_l.append(c.astype(state_C.dtype))
            ns_l.append(n.astype(state_n.dtype))
            ms_l.append(m.astype(state_m.dtype))
        y, buf = conv_ffn(rmsnorm(hp, ffn_norm[i]), w_ffn_up[i], ffn_conv_w[i], ffn_conv_b[i], w_ffn_down[i],
                          jnp.zeros((bp, CONV_WIDTH - 1, D_FF), hp.dtype))
        hp = hp + y
        convp_l.append(buf.astype(state_conv.dtype))
        y, buf = conv_ffn(rmsnorm(hs, ffn_norm[i]), w_ffn_up[i], ffn_conv_w[i], ffn_conv_b[i], w_ffn_down[i],
                          state_conv[i])
        hs = hs + y
        convs_l.append(buf.astype(state_conv.dtype))
    y_prompt = rmsnorm(hp, final_norm)
    y_sample = rmsnorm(hs, final_norm)
    return (y_prompt, y_sample, jnp.stack(kp_l), jnp.stack(vp_l), jnp.stack(ks_l), jnp.stack(vs_l),
            jnp.stack(cp_l), jnp.stack(np_l), jnp.stack(mp_l), jnp.stack(cs_l), jnp.stack(ns_l), jnp.stack(ms_l),
            jnp.stack(convp_l), jnp.stack(convs_l))
```

```python
import functools
import math

import numpy as np
import jax
import jax.numpy as jnp
from jax import lax
from jax.experimental import pallas as pl
from jax.experimental.pallas import tpu as pltpu

F32 = jnp.float32
BF16 = jnp.bfloat16

D_MODEL = 1024
N_HEADS = 16
HEAD_DIM = 64
MOBA_BLOCK = 256
MOBA_TOPK = 3
PAGE_SIZE = 128
REL_BUCKETS = 32
REL_MAX_DIST = 4096
M_HEADS = 4
M_QK_DIM = 128
M_V_DIM = 256
D_FF = 2816
EPS = 1e-6
NEG_INF = -1e30
LANES = 128
MLSTM_L = 256
VMEM_LIMIT = 56 * 1024 * 1024


def _cparams(sem, vmem=VMEM_LIMIT):
    return pltpu.CompilerParams(dimension_semantics=sem, vmem_limit_bytes=vmem)


def _dot(a, b):
    return jnp.dot(a, b, preferred_element_type=F32)


def _dot_nt(a, b):
    return lax.dot_general(a, b, (((1,), (1,)), ((), ())), preferred_element_type=F32)


def _dot_tn(a, b):
    return lax.dot_general(a, b, (((0,), (0,)), ((), ())), preferred_element_type=F32)


def _rmsnorm(x, g):
    ms = jnp.mean(x * x, axis=-1, keepdims=True)
    return x * lax.rsqrt(ms + EPS) * g


def _t5_bucket_np(dist):
    n = np.maximum(dist, 0).astype(np.int64)
    max_exact = REL_BUCKETS // 2
    nf = np.maximum(n, 1).astype(np.float64)
    large = max_exact + np.floor(
        np.log(nf / max_exact) / math.log(REL_MAX_DIST / max_exact) * (REL_BUCKETS - max_exact) + 1e-9
    ).astype(np.int64)
    large = np.minimum(large, REL_BUCKETS - 1)
    return np.where(n < max_exact, n, large).astype(np.int32)


def _norm_proj_kernel(x_ref, g_ref, w_ref, *out_refs, widths, scales, chunk):
    xn = _rmsnorm(x_ref[...], g_ref[...]).astype(BF16)
    off = 0
    for o_ref, n, sc in zip(out_refs, widths, scales):
        for c in range(0, n, chunk):
            y = _dot(xn, w_ref[:, off + c:off + c + chunk])
            if sc != 1.0:
                y = y * sc
            o_ref[:, c:c + chunk] = y.astype(o_ref.dtype)
        off += n


def _norm_proj(x, g, w_bf, widths, scales, dtypes, tm):
    m, d = x.shape
    n_total = w_bf.shape[1]
    assert sum(widths) == n_total and m % tm == 0
    kern = functools.partial(_norm_proj_kernel, widths=tuple(widths), scales=tuple(scales), chunk=512)
    return pl.pallas_call(
        kern,
        grid=(m // tm,),
        in_specs=[
            pl.BlockSpec((tm, d), lambda i: (i, 0)),
            pl.BlockSpec((1, d), lambda i: (0, 0)),
            pl.BlockSpec((d, n_total), lambda i: (0, 0)),
        ],
        out_specs=[pl.BlockSpec((tm, n), lambda i: (i, 0)) for n in widths],
        out_shape=[jax.ShapeDtypeStruct((m, n), dt) for n, dt in zip(widths, dtypes)],
        compiler_params=_cparams(("arbitrary",)),
        name="norm_proj",
    )(x, g, w_bf)


def _mlstm_in_kernel(x_ref, g_ref, w_ref, wg_ref, bg_ref, qk_ref, v_ref, o_ref, gt_ref, *, tm, pad_rows):
    xn = _rmsnorm(x_ref[...], g_ref[...]).astype(BF16)
    hk = M_HEADS * M_QK_DIM
    hv = M_HEADS * M_V_DIM
    qk_ref[:, 0:hk] = (_dot(xn, w_ref[:, 0:hk]) * (M_QK_DIM ** -0.5)).astype(qk_ref.dtype)
    qk_ref[:, hk:2 * hk] = _dot(xn, w_ref[:, hk:2 * hk]).astype(qk_ref.dtype)
    for c in range(0, hv, 512):
        v_ref[:, c:c + 512] = _dot(xn, w_ref[:, 2 * hk + c:2 * hk + c + 512]).astype(v_ref.dtype)
        o_ref[:, c:c + 512] = _dot(xn, w_ref[:, 2 * hk + hv + c:2 * hk + hv + c + 512])
    gates = _dot(xn, wg_ref[...])
    if pad_rows:
        gates = jnp.concatenate([gates, jnp.zeros((pad_rows, LANES), F32)], axis=0)
    gt = gates.T
    gt_ref[...] = gt[0:8, 0:tm] + bg_ref[...]


def _mlstm_in(x, g, w_bf, wg_bf, bg_col, tm, act_dtype):
    m, d = x.shape
    pad_rows = (-tm) % LANES
    kern = functools.partial(_mlstm_in_kernel, tm=tm, pad_rows=pad_rows)
    return pl.pallas_call(
        kern,
        grid=(m // tm,),
        in_specs=[
            pl.BlockSpec((tm, d), lambda i: (i, 0)),
            pl.BlockSpec((1, d), lambda i: (0, 0)),
            pl.BlockSpec(w_bf.shape, lambda i: (0, 0)),
            pl.BlockSpec(wg_bf.shape, lambda i: (0, 0)),
            pl.BlockSpec((8, 1), lambda i: (0, 0)),
        ],
        out_specs=[
            pl.BlockSpec((tm, 1024), lambda i: (i, 0)),
            pl.BlockSpec((tm, 1024), lambda i: (i, 0)),
            pl.BlockSpec((tm, 1024), lambda i: (i, 0)),
            pl.BlockSpec((8, tm), lambda i: (0, i)),
        ],
        out_shape=[
            jax.ShapeDtypeStruct((m, 1024), act_dtype),
            jax.ShapeDtypeStruct((m, 1024), act_dtype),
            jax.ShapeDtypeStruct((m, 1024), F32),
            jax.ShapeDtypeStruct((8, m), F32),
        ],
        compiler_params=_cparams(("arbitrary",)),
        name="mlstm_in",
    )(x, g, w_bf, wg_bf, bg_col)


def _bias_table_kernel(rb_ref, bk_ref, o_ref, *, bucket_sets):
    h = pl.program_id(0)
    for t, buckets in enumerate(bucket_sets):
        bk = bk_ref[t]
        acc = jnp.zeros(bk.shape, F32)
        for bucket in buckets:
            acc = jnp.where(bk == bucket, rb_ref[bucket, h], acc)
        o_ref[0, t] = acc


def _bias_tables(rel_bias, bucket_np, name):
    nt, r, c = bucket_np.shape
    bucket_sets = tuple(tuple(int(v) for v in np.unique(bucket_np[t])) for t in range(nt))
    kern = functools.partial(_bias_table_kernel, bucket_sets=bucket_sets)
    return pl.pallas_call(
        kern,
        grid=(N_HEADS,),
        in_specs=[
            pl.BlockSpec(memory_space=pltpu.SMEM),
            pl.BlockSpec((nt, r, c), lambda h: (0, 0, 0)),
        ],
        out_specs=pl.BlockSpec((1, nt, r, c), lambda h: (h, 0, 0, 0)),
        out_shape=jax.ShapeDtypeStruct((N_HEADS, nt, r, c), F32),
        compiler_params=_cparams(("arbitrary",)),
        name=name,
    )(rel_bias, jnp.asarray(bucket_np))


def _moba_prompt_kernel(q_ref, k_ref, v_ref, bias_ref, o_ref, kaug, vbf, *, seq, nb):
    blk = MOBA_BLOCK
    kf = k_ref[0]
    row_blk = lax.broadcasted_iota(jnp.int32, (seq, LANES), 0) // blk
    lane_s = lax.broadcasted_iota(jnp.int32, (seq, LANES), 1)
    kaug[:, 0:LANES] = kf.astype(BF16)
    kaug[:, LANES:2 * LANES] = jnp.where(row_blk == lane_s, 1.0, 0.0).astype(BF16)
    vbf[...] = v_ref[0].astype(BF16)
    km = jnp.mean(kf.reshape(nb, blk, LANES), axis=1)
    km_hi = km.astype(BF16)
    km_lo = (km - km_hi.astype(F32)).astype(BF16)

    lane_q = lax.broadcasted_iota(jnp.int32, (blk, LANES), 1)
    cand = lax.broadcasted_iota(jnp.int32, (nb, blk), 0)
    eye = (lax.broadcasted_iota(jnp.int32, (blk, blk), 0)
           == lax.broadcasted_iota(jnp.int32, (blk, blk), 1)).astype(BF16)
    causal = (lax.broadcasted_iota(jnp.int32, (blk, blk), 1)
              <= lax.broadcasted_iota(jnp.int32, (blk, blk), 0))

    for h in range(2):
        head_lanes = (lane_q // HEAD_DIM) == h

        def q_block(ob, carry, h=h, head_lanes=head_lanes):
            q0 = pl.multiple_of(ob * blk, blk)
            qb = q_ref[0, pl.ds(q0, blk), :]
            qm = jnp.where(head_lanes, qb, jnp.zeros_like(qb))
            sc = _dot_nt(km_hi, qm) + _dot_nt(km_lo, qm)
            valid = cand < ob
            scm = jnp.where(valid, sc, -jnp.inf)
            cnt = jnp.zeros((nb, blk), F32)
            for m in range(nb):
                row = scm[m:m + 1, :]
                beats = jnp.where(row > scm, 1.0, jnp.where((row == scm) & (cand > m), 1.0, 0.0))
                cnt = cnt + beats
            pen_t = jnp.where(valid & (cnt < MOBA_TOPK), 0.0, NEG_INF).astype(BF16)
            pen_t = jnp.concatenate([pen_t, jnp.zeros((LANES - nb, blk), BF16)], axis=0)
            pen = _dot_nt(eye, pen_t).astype(BF16)
            qaug = jnp.concatenate([qm, pen], axis=1)

            k_own = kaug[pl.ds(q0, blk), 0:LANES]
            s = _dot_nt(qm, k_own) + bias_ref[h, 0]
            s = jnp.where(causal, s, NEG_INF)
            m_i = jnp.max(s, axis=-1, keepdims=True)
            p = jnp.exp(s - m_i)
            l_i = jnp.sum(p, axis=-1, keepdims=True)
            acc = _dot(p.astype(BF16), vbf[pl.ds(q0, blk), :])

            def k_block(n, st):
                m_i, l_i, acc = st
                k0 = pl.multiple_of(n * blk, blk)
                s = _dot_nt(qaug, kaug[pl.ds(k0, blk), :]) + bias_ref[h, ob - n]
                m_new = jnp.maximum(m_i, jnp.max(s, axis=-1, keepdims=True))
                alpha = jnp.exp(m_i - m_new)
                p = jnp.exp(s - m_new)
                l_new = alpha * l_i + jnp.sum(p, axis=-1, keepdims=True)
                acc_new = alpha * acc + _dot(p.astype(BF16), vbf[pl.ds(k0, blk), :])
                return m_new, l_new, acc_new

            m_i, l_i, acc = lax.fori_loop(0, ob, k_block, (m_i, l_i, acc))
            out = (acc / l_i).astype(o_ref.dtype)
            if h == 0:
                o_ref[0, pl.ds(q0, blk), :] = out
            else:
                o_ref[0, pl.ds(q0, blk), :] = jnp.where(head_lanes, out, o_ref[0, pl.ds(q0, blk), :])
            return carry

        lax.fori_loop(0, nb, q_block, 0)


def _moba_prompt(q, k, v, bias_tiles):
    b, s, d = q.shape
    nb = s // MOBA_BLOCK
    hp = d // LANES
    kern = functools.partial(_moba_prompt_kernel, seq=s, nb=nb)
    return pl.pallas_call(
        kern,
        grid=(hp, b),
        in_specs=[
            pl.BlockSpec((1, s, LANES), lambda p, i: (i, 0, p)),
            pl.BlockSpec((1, s, LANES), lambda p, i: (i, 0, p)),
            pl.BlockSpec((1, s, LANES), lambda p, i: (i, 0, p)),
            pl.BlockSpec((2, nb, MOBA_BLOCK, MOBA_BLOCK), lambda p, i: (p, 0, 0, 0)),
        ],
        out_specs=pl.BlockSpec((1, s, LANES), lambda p, i: (i, 0, p)),
        out_shape=jax.ShapeDtypeStruct((b, s, d), BF16),
        scratch_shapes=[pltpu.VMEM((s, 2 * LANES), BF16), pltpu.VMEM((s, LANES), BF16)],
        compiler_params=_cparams(("arbitrary", "arbitrary")),
        name="moba_prompt",
    )(q, k, v, bias_tiles)


def _ffn_kernel(*refs, tm, tiles_per_seq, seq_mode, final):
    if seq_mode:
        (x_ref, a_ref, wo_ref, g_ref, wup_ref, cw_ref, cb_ref, wdn_ref, gf_ref,
         y_ref, cs_ref, carry) = refs
    else:
        (x_ref, a_ref, wo_ref, g_ref, wup_ref, cw_ref, cb_ref, wdn_ref, gf_ref, st0_ref, st1_ref,
         y_ref, cs0_ref, cs1_ref) = refs
    i = pl.program_id(0)
    x1 = x_ref[...] + _dot(a_ref[...], wo_ref[...])
    xn = _rmsnorm(x1, g_ref[...]).astype(BF16)
    if seq_mode:
        @pl.when(i % tiles_per_seq == 0)
        def _():
            carry[...] = jnp.zeros_like(carry)
        row = lax.broadcasted_iota(jnp.int32, (tm, 256), 0)
    acc = jnp.zeros((tm, D_MODEL), F32)
    cf = 256
    for c in range(0, D_FF, cf):
        ug = _dot(xn, wup_ref[:, c:c + cf])
        uv = _dot(xn, wup_ref[:, D_FF + c:D_FF + c + cf])
        if seq_mode:
            c0 = carry[0:1, c:c + cf]
            c1 = carry[1:2, c:c + cf]
            p1 = jnp.where(row == 0, c1, pltpu.roll(ug, 1, axis=0))
            p2 = jnp.where(row == 0, c0, jnp.where(row == 1, c1, pltpu.roll(ug, 2, axis=0)))
            last2 = ug[tm - 2:tm, :]
            carry[0:2, c:c + cf] = last2
            cs_ref[0, :, c:c + cf] = last2
        else:
            p2 = st0_ref[:, c:c + cf]
            p1 = st1_ref[:, c:c + cf]
            cs0_ref[:, c:c + cf] = p1
            cs1_ref[:, c:c + cf] = ug
        conv = (cb_ref[:, c:c + cf] + cw_ref[0:1, c:c + cf] * p2 + cw_ref[1:2, c:c + cf] * p1
                + cw_ref[2:3, c:c + cf] * ug)
        hmid = (conv * (1.0 / (1.0 + jnp.exp(-conv))) * uv).astype(BF16)
        acc = acc + _dot(hmid, wdn_ref[c:c + cf, :])
    y = x1 + acc
    if final:
        y = _rmsnorm(y, gf_ref[...])
    y_ref[...] = y


def _ffn(x, a, wo_bf, g, wup_bf, cw, cb, wdn_bf, gf, *, tm, seq_len, state=None, final=False):
    m, d = x.shape
    seq_mode = state is None
    tiles_per_seq = (seq_len // tm) if seq_mode else 1
    kern = functools.partial(_ffn_kernel, tm=tm, tiles_per_seq=tiles_per_seq, seq_mode=seq_mode, final=final)
    const = lambda i: (0, 0)
    resident = functools.partial(pl.BlockSpec, pipeline_mode=pl.Buffered(1))
    in_specs = [
        pl.BlockSpec((tm, d), lambda i: (i, 0)),
        pl.BlockSpec((tm, d), lambda i: (i, 0)),
        resident(wo_bf.shape, const),
        pl.BlockSpec((1, d), const),
        resident(wup_bf.shape, const),
        pl.BlockSpec((3, D_FF), const),
        pl.BlockSpec((1, D_FF), const),
        resident(wdn_bf.shape, const),
        pl.BlockSpec((1, d), const),
    ]
    args = [x, a, wo_bf, g, wup_bf, cw, cb, wdn_bf, gf]
    scratch = []
    if seq_mode:
        nseq = m // seq_len
        cs_shapes = [jax.ShapeDtypeStruct((nseq, 2, D_FF), F32)]
        cs_specs = [pl.BlockSpec((1, 2, D_FF), lambda i: (i // tiles_per_seq, 0, 0))]
        scratch = [pltpu.VMEM((8, D_FF), F32)]
    else:
        in_specs += [pl.BlockSpec((tm, D_FF), lambda i: (i, 0))] * 2
        args += [state[:, 0], state[:, 1]]
        cs_shapes = [jax.ShapeDtypeStruct((m, D_FF), F32)] * 2
        cs_specs = [pl.BlockSpec((tm, D_FF), lambda i: (i, 0))] * 2
    outs = pl.pallas_call(
        kern,
        grid=(m // tm,),
        in_specs=in_specs,
        out_specs=[pl.BlockSpec((tm, d), lambda i: (i, 0))] + cs_specs,
        out_shape=[jax.ShapeDtypeStruct((m, d), F32)] + cs_shapes,
        scratch_shapes=scratch,
        compiler_params=_cparams(("arbitrary",)),
        name="ffn_seq" if seq_mode else "ffn_step",
    )(*args)
    if seq_mode:
        return outs[0], outs[1]
    return outs[0], jnp.stack([outs[1], outs[2]], axis=1)


def _log_sigmoid(x):
    return jnp.minimum(x, 0.0) - jnp.log1p(jnp.exp(-jnp.abs(x)))


def _mlstm_seq_kernel(q_ref, k_ref, v_ref, o_ref, gt_ref, gain_ref,
                      h_ref, c_out, n_out, m_out, c_s, n_s, m_s, *, seq):
    L = MLSTM_L
    hd = pl.program_id(1)
    c_s[...] = jnp.zeros_like(c_s)
    n_s[...] = jnp.zeros_like(n_s)
    m_s[...] = jnp.zeros_like(m_s)
    lane = lax.broadcasted_iota(jnp.int32, (1, L), 1)
    tri = (lax.broadcasted_iota(jnp.int32, (L, L), 1) <= lax.broadcasted_iota(jnp.int32, (L, L), 0))
    diag = (lax.broadcasted_iota(jnp.int32, (L, L), 1) == lax.broadcasted_iota(jnp.int32, (L, L), 0))

    def chunk(ci, carry):
        t0 = pl.multiple_of(ci * L, L)
        ig = gt_ref[pl.ds(hd, 1), pl.ds(t0, L)]
        fg = gt_ref[pl.ds(M_HEADS + hd, 1), pl.ds(t0, L)]
        lf = _log_sigmoid(fg)
        bc = lf
        sh = 1
        while sh < L:
            bc = bc + jnp.where(lane >= sh, pltpu.roll(bc, sh, axis=1), 0.0)
            sh *= 2
        u = ig - bc
        m_prev = m_s[0:1, 0:1]
        cmax_c = jnp.max(jnp.where(tri, u, -jnp.inf), axis=-1, keepdims=True)
        bc_c = jnp.sum(jnp.where(tri, lf, 0.0), axis=-1, keepdims=True)
        ig_c = jnp.sum(jnp.where(diag, ig, 0.0), axis=-1, keepdims=True)
        big_m = jnp.maximum(m_prev, cmax_c)
        dmat = jnp.exp(jnp.where(tri, u - big_m, -jnp.inf))
        inter = jnp.exp(m_prev - big_m)
        mt_c = bc_c + big_m

        q = q_ref[0, pl.ds(t0, L), :]
        k = k_ref[0, pl.ds(t0, L), :]
        v = v_ref[0, pl.ds(t0, L), :]
        sm = _dot_nt(q, k) * dmat
        num = _dot(sm.astype(BF16), v) + inter * _dot(q, c_s[...].astype(BF16))
        qn = jnp.sum(q.astype(F32) * n_s[...], axis=-1, keepdims=True)
        den = jnp.sum(sm, axis=-1, keepdims=True) + inter * qn
        hout = num / jnp.maximum(jnp.abs(den), jnp.exp(-mt_c))

        bc_last = jnp.sum(lf, axis=-1, keepdims=True)
        m_new = bc_last + jnp.maximum(m_prev, jnp.max(u, axis=-1, keepdims=True))
        decay = jnp.exp(bc_last + m_prev - m_new)
        w_r = jnp.exp(u + bc_last - m_new)
        w_c = jnp.exp(ig_c - bc_c + bc_last - m_new)
        vw = (v.astype(F32) * w_c).astype(BF16)
        c_s[...] = decay * c_s[...] + _dot_tn(k, vw)
        w8 = jnp.broadcast_to(w_r, (8, L)).astype(BF16)
        n_s[...] = decay * n_s[...] + _dot(w8, k)[0:1, :]
        m_s[...] = jnp.broadcast_to(m_new, m_s.shape)

        hn = hout * lax.rsqrt(jnp.mean(hout * hout, axis=-1, keepdims=True) + EPS)
        o = o_ref[0, pl.ds(t0, L), :]
        hn = hn * gain_ref[...] * (1.0 / (1.0 + jnp.exp(-o)))
        h_ref[0, pl.ds(t0, L), :] = hn.astype(h_ref.dtype)
        return carry

    lax.fori_loop(0, seq // L, chunk, 0)
    c_out[0, 0] = c_s[...]
    n_out[0, 0] = n_s[...]
    m_out[0, 0] = m_s[0:1, :]


def _mlstm_seq(qk, v, o, gt, gain, b, s):
    kern = functools.partial(_mlstm_seq_kernel, seq=s)
    qk3 = qk.reshape(b, s, 2 * M_HEADS * M_QK_DIM)
    v3 = v.reshape(b, s, M_HEADS * M_V_DIM)
    o3 = o.reshape(b, s, M_HEADS * M_V_DIM)
    return pl.pallas_call(
        kern,
        grid=(b, M_HEADS),
        in_specs=[
            pl.BlockSpec((1, s, M_QK_DIM), lambda i, h: (i, 0, h)),
            pl.BlockSpec((1, s, M_QK_DIM), lambda i, h: (i, 0, M_HEADS + h)),
            pl.BlockSpec((1, s, M_V_DIM), lambda i, h: (i, 0, h)),
            pl.BlockSpec((1, s, M_V_DIM), lambda i, h: (i, 0, h)),
            pl.BlockSpec((8, s), lambda i, h: (0, i)),
            pl.BlockSpec((1, M_V_DIM), lambda i, h: (0, h)),
        ],
        out_specs=[
            pl.BlockSpec((1, s, M_V_DIM), lambda i, h: (i, 0, h)),
            pl.BlockSpec((1, 1, M_QK_DIM, M_V_DIM), lambda i, h: (i, h, 0, 0)),
            pl.BlockSpec((1, 1, 1, M_QK_DIM), lambda i, h: (i, h, 0, 0)),
            pl.BlockSpec((1, 1, 1, LANES), lambda i, h: (i, h, 0, 0)),
        ],
        out_shape=[
            jax.ShapeDtypeStruct((b, s, M_HEADS * M_V_DIM), BF16),
            jax.ShapeDtypeStruct((b, M_HEADS, M_QK_DIM, M_V_DIM), F32),
            jax.ShapeDtypeStruct((b, M_HEADS, 1, M_QK_DIM), F32),
            jax.ShapeDtypeStruct((b, M_HEADS, 1, LANES), F32),
        ],
        scratch_shapes=[pltpu.VMEM((M_QK_DIM, M_V_DIM), F32), pltpu.VMEM((1, M_QK_DIM), F32),
                        pltpu.VMEM((8, LANES), F32)],
        compiler_params=_cparams(("arbitrary", "arbitrary")),
        name="mlstm_seq",
    )(qk3, qk3, v3, o3, gt, gain)


def _mlstm_step_kernel(qk_ref, v_ref, o_ref, g_ref, gain_ref, c_ref, n_ref, m_ref,
                       h_ref, c_out, n_out, m_out):
    eye = (lax.broadcasted_iota(jnp.int32, (M_QK_DIM, M_QK_DIM), 0)
           == lax.broadcasted_iota(jnp.int32, (M_QK_DIM, M_QK_DIM), 1))
    hk = M_HEADS * M_QK_DIM
    for h in range(M_HEADS):
        q = qk_ref[0, :, h * M_QK_DIM:(h + 1) * M_QK_DIM]
        k = qk_ref[0, :, hk + h * M_QK_DIM:hk + (h + 1) * M_QK_DIM]
        v = v_ref[0, :, h * M_V_DIM:(h + 1) * M_V_DIM]
        o = o_ref[0, :, h * M_V_DIM:(h + 1) * M_V_DIM]
        ig = g_ref[0, :, h:h + 1]
        lf = _log_sigmoid(g_ref[0, :, M_HEADS + h:M_HEADS + h + 1])
        m_prev = m_ref[0, :, h:h + 1]
        c = c_ref[0, h]
        n = n_ref[0, h]
        q_col = jnp.sum(jnp.where(eye, q, 0.0), axis=-1, keepdims=True)
        k_col = jnp.sum(jnp.where(eye, k, 0.0), axis=-1, keepdims=True)
        g = lf + m_prev
        mt = jnp.maximum(g, ig)
        dm = jnp.exp(ig - mt)
        inter = jnp.exp(g - mt)
        sm = jnp.sum(q * k, axis=-1, keepdims=True) * dm
        qc = jnp.sum(q_col * c, axis=0, keepdims=True)
        num = sm * v + inter * qc
        den = sm + inter * jnp.sum(q * n, axis=-1, keepdims=True)
        hout = num / jnp.maximum(jnp.abs(den), jnp.exp(-mt))
        c_out[0, h] = inter * c + dm * (k_col * v)
        n_out[0, h] = inter * n + dm * k
        m_out[0, :, h:h + 1] = mt
        hn = hout * lax.rsqrt(jnp.mean(hout * hout, axis=-1, keepdims=True) + EPS)
        hn = hn * gain_ref[:, h * M_V_DIM:(h + 1) * M_V_DIM] * (1.0 / (1.0 + jnp.exp(-o)))
        h_ref[0, :, h * M_V_DIM:(h + 1) * M_V_DIM] = hn.astype(h_ref.dtype)


def _mlstm_step(qk, v, o, g_rows, gain, c0, n0, m0):
    nb = qk.shape[0]
    hv = M_HEADS * M_V_DIM
    row3 = lambda a: a.reshape(nb, 1, a.shape[-1])
    spec3 = lambda w: pl.BlockSpec((1, 1, w), lambda i: (i, 0, 0))
    return pl.pallas_call(
        _mlstm_step_kernel,
        grid=(nb,),
        in_specs=[
            spec3(1024), spec3(hv), spec3(hv), spec3(8),
            pl.BlockSpec((1, hv), lambda i: (0, 0)),
            pl.BlockSpec((1, M_HEADS, M_QK_DIM, M_V_DIM), lambda i: (i, 0, 0, 0)),
            pl.BlockSpec((1, M_HEADS, 1, M_QK_DIM), lambda i: (i, 0, 0, 0)),
            spec3(M_HEADS),
        ],
        out_specs=[
            spec3(hv),
            pl.BlockSpec((1, M_HEADS, M_QK_DIM, M_V_DIM), lambda i: (i, 0, 0, 0)),
            pl.BlockSpec((1, M_HEADS, 1, M_QK_DIM), lambda i: (i, 0, 0, 0)),
            spec3(M_HEADS),
        ],
        out_shape=[
            jax.ShapeDtypeStruct((nb, 1, hv), BF16),
            jax.ShapeDtypeStruct((nb, M_HEADS, M_QK_DIM, M_V_DIM), F32),
            jax.ShapeDtypeStruct((nb, M_HEADS, 1, M_QK_DIM), F32),
            jax.ShapeDtypeStruct((nb, 1, M_HEADS), F32),
        ],
        compiler_params=_cparams(("arbitrary",)),
        name="mlstm_step",
    )(row3(qk), row3(v), row3(o), row3(g_rows), gain, c0,
      n0.reshape(nb, M_HEADS, 1, M_QK_DIM), row3(m0))


KMEAN_BUFS = 4


def _kmean_kernel(pt_ref, ck_ref, o_ref, buf, sem, *, n_seq, nblk):
    b = pl.program_id(0)
    total = n_seq * nblk
    ppb = MOBA_BLOCK // PAGE_SIZE

    def copies(g):
        slot = g % KMEAN_BUFS
        bb = g // nblk
        n = g % nblk
        return [pltpu.make_async_copy(ck_ref.at[pt_ref[bb, ppb * n + j]], buf.at[slot, j], sem.at[slot, j])
                for j in range(ppb)]

    @pl.when(b == 0)
    def _():
        for g in range(KMEAN_BUFS - 1):
            for cp in copies(g):
                cp.start()

    lane = lax.broadcasted_iota(jnp.int32, (N_HEADS * HEAD_DIM, LANES), 1)

    def body(n, acc):
        g = b * nblk + n
        for cp in copies(g):
            cp.wait()

        @pl.when(g + KMEAN_BUFS - 1 < total)
        def _():
            for cp in copies(g + KMEAN_BUFS - 1):
                cp.start()

        slot = g % KMEAN_BUFS
        x = buf[slot, 0]
        for j in range(1, ppb):
            x = x + buf[slot, j]
        ssum = jnp.sum(x.reshape(N_HEADS * HEAD_DIM, PAGE_SIZE), axis=-1, keepdims=True)
        return jnp.where(lane == n, ssum, acc)

    acc = lax.fori_loop(0, nblk, body, jnp.zeros((N_HEADS * HEAD_DIM, LANES), F32))
    o_ref[0] = acc * (1.0 / MOBA_BLOCK)


def _kmean(page_table, ck_t, nblk):
    n_seq = page_table.shape[0]
    kern = functools.partial(_kmean_kernel, n_seq=n_seq, nblk=nblk)
    ppb = MOBA_BLOCK // PAGE_SIZE
    return pl.pallas_call(
        kern,
        grid_spec=pltpu.PrefetchScalarGridSpec(
            num_scalar_prefetch=1,
            grid=(n_seq,),
            in_specs=[pl.BlockSpec(memory_space=pl.ANY)],
            out_specs=pl.BlockSpec((1, N_HEADS * HEAD_DIM, LANES), lambda b, pt: (b, 0, 0)),
            scratch_shapes=[pltpu.VMEM((KMEAN_BUFS, ppb, N_HEADS, HEAD_DIM, PAGE_SIZE), F32),
                            pltpu.SemaphoreType.DMA((KMEAN_BUFS, ppb))],
        ),
        out_shape=jax.ShapeDtypeStruct((n_seq, N_HEADS * HEAD_DIM, LANES), F32),
        compiler_params=_cparams(("arbitrary",)),
        name="kmean",
    )(page_table, ck_t)


def _select_kernel(qt_ref, km_ref, sel_ref, *, nblk):
    b = pl.program_id(0)
    lane_q = lax.broadcasted_iota(jnp.int32, (N_HEADS * HEAD_DIM, LANES), 1)
    q_col = jnp.sum(jnp.where(lane_q == b, qt_ref[...], 0.0), axis=-1, keepdims=True)
    prod = (km_ref[0] * q_col).reshape(N_HEADS, HEAD_DIM, LANES)
    sc = jnp.sum(prod, axis=1)
    lane = lax.broadcasted_iota(jnp.int32, (N_HEADS, LANES), 1)
    lane_f = lane.astype(F32)
    sc = jnp.where(lane < nblk, sc, -jnp.inf)
    out = jnp.zeros((N_HEADS, LANES), jnp.int32)
    for r in range(MOBA_TOPK):
        mx = jnp.max(sc, axis=-1, keepdims=True)
        idx = jnp.min(jnp.where(sc == mx, lane_f, float(LANES)), axis=-1, keepdims=True)
        out = jnp.where(lane == r, idx.astype(jnp.int32), out)
        sc = jnp.where(lane_f == idx, -jnp.inf, sc)
    sel_ref[0] = out


def _select(qt, kmean_t, nblk):
    n_seq = kmean_t.shape[0]
    kern = functools.partial(_select_kernel, nblk=nblk)
    return pl.pallas_call(
        kern,
        grid=(n_seq,),
        in_specs=[
            pl.BlockSpec(qt.shape, lambda b: (0, 0)),
            pl.BlockSpec((1, N_HEADS * HEAD_DIM, LANES), lambda b: (b, 0, 0)),
        ],
        out_specs=pl.BlockSpec((1, N_HEADS, LANES), lambda b: (b, 0, 0)),
        out_shape=jax.ShapeDtypeStruct((n_seq, N_HEADS, LANES), jnp.int32),
        compiler_params=_cparams(("arbitrary",)),
        name="moba_select",
    )(qt, kmean_t)


def _decode_attn_kernel(sel_ref, pt_ref, ck_ref, cv_ref, qt_ref, kt_ref, vt_ref, bias_ref, o_ref,
                        kbuf, vbuf, sem, *, n_pages):
    b = pl.program_id(0)
    ppb = MOBA_BLOCK // PAGE_SIZE
    n_t = MOBA_TOPK * ppb
    self_row = n_pages

    def copies(h, t):
        r, j = divmod(t, ppb)
        lp = ppb * sel_ref[b, h * MOBA_TOPK + r] + j
        page = pt_ref[b, lp]
        return (pltpu.make_async_copy(ck_ref.at[page, h], kbuf.at[h, t], sem.at[0, h, t]),
                pltpu.make_async_copy(cv_ref.at[page, h], vbuf.at[h, t], sem.at[1, h, t]))

    for h in range(N_HEADS):
        for t in range(n_t):
            for cp in copies(h, t):
                cp.start()

    @pl.when(b == 0)
    def _():
        o_ref[...] = jnp.zeros_like(o_ref)

    lane_b = lax.broadcasted_iota(jnp.int32, (N_HEADS * HEAD_DIM, LANES), 1) == b
    pick = lambda ref: jnp.sum(jnp.where(lane_b, ref[...], 0.0), axis=-1, keepdims=True)
    q_col = pick(qt_ref)
    k_col = pick(kt_ref)
    v_col = pick(vt_ref)
    lane_o = lax.broadcasted_iota(jnp.int32, (HEAD_DIM, LANES), 1) == b

    for h in range(N_HEADS):
        rows = slice(h * HEAD_DIM, (h + 1) * HEAD_DIM)
        qc = q_col[rows]
        s_self = (jnp.sum(qc * k_col[rows], axis=0, keepdims=True)
                  + bias_ref[h, self_row:self_row + 1, 0:1])
        s_list = []
        for t in range(n_t):
            ck, cv = copies(h, t)
            ck.wait()
            cv.wait()
            r, j = divmod(t, ppb)
            lp = ppb * sel_ref[b, h * MOBA_TOPK + r] + j
            s_t = jnp.sum(kbuf[h, t] * qc, axis=0, keepdims=True) + bias_ref[h, pl.ds(lp, 1), :]
            s_list.append(s_t)
        m = s_self
        for s_t in s_list:
            m = jnp.maximum(m, jnp.max(s_t, axis=-1, keepdims=True))
        p_self = jnp.exp(s_self - m)
        l = p_self
        acc = jnp.zeros((HEAD_DIM, LANES), F32)
        for t, s_t in enumerate(s_list):
            p = jnp.exp(s_t - m)
            l = l + jnp.sum(p, axis=-1, keepdims=True)
            acc = acc + vbuf[h, t] * p
        o_col = (jnp.sum(acc, axis=-1, keepdims=True) + p_self * v_col[rows]) / l
        o_ref[rows, :] = jnp.where(lane_o, o_col, o_ref[rows, :])


def _decode_attn(sel, page_table, ck_t, cv_t, qt, kt, vt, bias_s):
    n_seq, n_pages = page_table.shape
    kern = functools.partial(_decode_attn_kernel, n_pages=n_pages)
    n_t = MOBA_TOPK * (MOBA_BLOCK // PAGE_SIZE)
    full = lambda a: pl.BlockSpec(a.shape, lambda b, s, p: (0,) * a.ndim)
    return pl.pallas_call(
        kern,
        grid_spec=pltpu.PrefetchScalarGridSpec(
            num_scalar_prefetch=2,
            grid=(n_seq,),
            in_specs=[pl.BlockSpec(memory_space=pl.ANY), pl.BlockSpec(memory_space=pl.ANY),
                      full(qt), full(kt), full(vt), full(bias_s)],
            out_specs=pl.BlockSpec((N_HEADS * HEAD_DIM, LANES), lambda b, s, p: (0, 0)),
            scratch_shapes=[pltpu.VMEM((N_HEADS, n_t, HEAD_DIM, PAGE_SIZE), F32),
                            pltpu.VMEM((N_HEADS, n_t, HEAD_DIM, PAGE_SIZE), F32),
                            pltpu.SemaphoreType.DMA((2, N_HEADS, n_t))],
        ),
        out_shape=jax.ShapeDtypeStruct((N_HEADS * HEAD_DIM, LANES), F32),
        compiler_params=_cparams(("arbitrary",)),
        name="moba_decode",
    )(sel, page_table, ck_t, cv_t, qt, kt, vt, bias_s)


def _qkv_step_kernel(x_ref, g_ref, w_ref, k_ref, v_ref, qt_ref, kt_ref, vt_ref, *, nb):
    xn = _rmsnorm(x_ref[...], g_ref[...]).astype(BF16)
    d = D_MODEL
    pad = jnp.zeros((LANES - nb, d), F32)
    q = _dot(xn, w_ref[:, 0:d]) * (HEAD_DIM ** -0.5)
    k = _dot(xn, w_ref[:, d:2 * d])
    v = _dot(xn, w_ref[:, 2 * d:3 * d])
    k_ref[...] = k
    v_ref[...] = v
    qt_ref[...] = jnp.concatenate([q, pad], axis=0).T
    kt_ref[...] = jnp.concatenate([k, pad], axis=0).T
    vt_ref[...] = jnp.concatenate([v, pad], axis=0).T


def _qkv_step(x, g, w_bf):
    nb, d = x.shape
    kern = functools.partial(_qkv_step_kernel, nb=nb)
    full = lambda shape: pl.BlockSpec(shape, lambda i: (0,) * len(shape))
    return pl.pallas_call(
        kern,
        grid=(1,),
        in_specs=[full((nb, d)), full((1, d)), full(w_bf.shape)],
        out_specs=[full((nb, d)), full((nb, d)), full((d, LANES)), full((d, LANES)), full((d, LANES))],
        out_shape=[jax.ShapeDtypeStruct((nb, d), F32), jax.ShapeDtypeStruct((nb, d), F32),
                   jax.ShapeDtypeStruct((d, LANES), F32), jax.ShapeDtypeStruct((d, LANES), F32),
                   jax.ShapeDtypeStruct((d, LANES), F32)],
        compiler_params=_cparams(("arbitrary",)),
        name="qkv_step",
    )(x, g, w_bf)


def kernel(x_prompt, x_sample, cache_k, cache_v, state_C, state_n, state_m, state_conv, page_table, rel_bias, attn_norm, w_qkv, w_attn_out, mlstm_norm, w_mlstm_in, b_mlstm_gate, mlstm_head_gain, w_mlstm_out, ffn_norm, w_ffn_up, ffn_conv_w, ffn_conv_b, w_ffn_down, final_norm):
    bp, sp, d = x_prompt.shape
    bs, ts, _ = x_sample.shape
    assert ts == 1 and d == D_MODEL and sp % MOBA_BLOCK == 0 and sp % MLSTM_L == 0
    n_pages = page_table.shape[1]
    assert (n_pages * PAGE_SIZE) % MOBA_BLOCK == 0
    nblk = n_pages * PAGE_SIZE // MOBA_BLOCK
    assert MOBA_TOPK <= nblk <= LANES
    mp = bp * sp
    nb_p = sp // MOBA_BLOCK
    hk = M_HEADS * M_QK_DIM
    hv = M_HEADS * M_V_DIM
    row = lambda a: a.reshape(1, -1)

    ii = np.arange(MOBA_BLOCK)
    bk_prompt = np.stack([_t5_bucket_np(t * MOBA_BLOCK + ii[:, None] - ii[None, :]) for t in range(nb_p)])
    past = n_pages * PAGE_SIZE
    kpos = np.arange(n_pages * PAGE_SIZE).reshape(n_pages, PAGE_SIZE)
    bk_step = np.concatenate([_t5_bucket_np(past - kpos), np.zeros((8, PAGE_SIZE), np.int32)])[None]
    bias_p = _bias_tables(rel_bias, bk_prompt, "bias_prompt")
    bias_s = _bias_tables(rel_bias, bk_step, "bias_step")[:, 0]

    xp = x_prompt.reshape(mp, d)
    xs = x_sample.reshape(bs, d)

    wqkv = w_qkv[0].astype(BF16)
    wo0 = w_attn_out[0].astype(BF16)
    q_p, k_p, v_p = _norm_proj(xp, row(attn_norm[0]), wqkv, (d, d, d), (HEAD_DIM ** -0.5, 1.0, 1.0),
                               (BF16, F32, F32), tm=512)
    attn_p = _moba_prompt(q_p.reshape(bp, sp, d), k_p.reshape(bp, sp, d), v_p.reshape(bp, sp, d), bias_p)

    k_s, v_s, qt_s, kt_s, vt_s = _qkv_step(xs, row(attn_norm[0]), wqkv)
    ck_t = jnp.transpose(cache_k[0], (0, 2, 3, 1))
    cv_t = jnp.transpose(cache_v[0], (0, 2, 3, 1))
    kmean_t = _kmean(page_table, ck_t, nblk)
    sel = _select(qt_s, kmean_t, nblk)[:, :, :MOBA_TOPK].reshape(bs, N_HEADS * MOBA_TOPK)
    attn_s_t = _decode_attn(sel, page_table, ck_t, cv_t, qt_s, kt_s, vt_s, bias_s)
    attn_s = attn_s_t.T[:bs].astype(BF16)

    ffn_w = lambda i: (row(ffn_norm[i]), w_ffn_up[i].astype(BF16), ffn_conv_w[i], row(ffn_conv_b[i]),
                       w_ffn_down[i].astype(BF16))
    g0, wup0, cw0, cb0, wdn0 = ffn_w(0)
    gf = row(final_norm)
    hp1, conv_p0 = _ffn(xp, attn_p.reshape(mp, d), wo0, g0, wup0, cw0, cb0, wdn0, gf, tm=512, seq_len=sp)
    hs1, conv_s0 = _ffn(xs, attn_s, wo0, g0, wup0, cw0, cb0, wdn0, gf, tm=bs, seq_len=1, state=state_conv[0])

    w_in = w_mlstm_in[0]
    w_in_bf = w_in[:, :2 * hk + 2 * hv].astype(BF16)
    wg_bf = jnp.pad(w_in[:, 2 * hk + 2 * hv:], ((0, 0), (0, LANES - 2 * M_HEADS))).astype(BF16)
    bg_col = b_mlstm_gate[0].reshape(2 * M_HEADS, 1)
    wo1 = w_mlstm_out[0].astype(BF16)
    gain = row(mlstm_head_gain[0])

    qk_p, vv_p, oo_p, gt_p = _mlstm_in(hp1, row(mlstm_norm[0]), w_in_bf, wg_bf, bg_col, tm=512, act_dtype=BF16)
    hc_p, c_p, n_p, m_p = _mlstm_seq(qk_p, vv_p, oo_p, gt_p, gain, bp, sp)

    qk_s, vv_s, oo_s, gt_s = _mlstm_in(hs1, row(mlstm_norm[0]), w_in_bf, wg_bf, bg_col, tm=bs, act_dtype=F32)
    hc_s, c_s, n_s, m_s = _mlstm_step(qk_s, vv_s, oo_s, gt_s.T, gain, state_C[0], state_n[0], state_m[0])

    g1, wup1, cw1, cb1, wdn1 = ffn_w(1)
    y_p, conv_p1 = _ffn(hp1, hc_p.reshape(mp, d), wo1, g1, wup1, cw1, cb1, wdn1, gf, tm=512, seq_len=sp,
                        final=True)
    y_s, conv_s1 = _ffn(hs1, hc_s.reshape(bs, d), wo1, g1, wup1, cw1, cb1, wdn1, gf, tm=bs, seq_len=1,
                        state=state_conv[1], final=True)

    kv5 = lambda a, n: a.reshape(1, n, -1, N_HEADS, HEAD_DIM)
    return (y_p.reshape(bp, sp, d), y_s.reshape(bs, ts, d),
            kv5(k_p, bp), kv5(v_p, bp), kv5(k_s, bs), kv5(v_s, bs),
            c_p[None], n_p.reshape(1, bp, M_HEADS, M_QK_DIM), m_p[:, :, 0, 0][None],
            c_s[None], n_s.reshape(1, bs, M_HEADS, M_QK_DIM), m_s.reshape(1, bs, M_HEADS),
            jnp.stack([conv_p0, conv_p1]), jnp.stack([conv_s0, conv_s1]))
```

```python
import functools
import math

import numpy as np
import jax
import jax.numpy as jnp
from jax import lax
from jax.experimental import pallas as pl
from jax.experimental.pallas import tpu as pltpu

F32 = jnp.float32
BF16 = jnp.bfloat16

D_MODEL = 1024
N_HEADS = 16
HEAD_DIM = 64
MOBA_BLOCK = 256
MOBA_TOPK = 3
PAGE_SIZE = 128
REL_BUCKETS = 32
REL_MAX_DIST = 4096
M_HEADS = 4
M_QK_DIM = 128
M_V_DIM = 256
D_FF = 2816
EPS = 1e-6
NEG_INF = -1e30
LANES = 128
MLSTM_L = 256
VMEM_LIMIT = 56 * 1024 * 1024


def _cparams(sem, vmem=VMEM_LIMIT):
    return pltpu.CompilerParams(dimension_semantics=sem, vmem_limit_bytes=vmem)


def _dot(a, b):
    return jnp.dot(a, b, preferred_element_type=F32)


def _dot_nt(a, b):
    return lax.dot_general(a, b, (((1,), (1,)), ((), ())), preferred_element_type=F32)


def _dot_tn(a, b):
    return lax.dot_general(a, b, (((0,), (0,)), ((), ())), preferred_element_type=F32)


def _rmsnorm(x, g):
    ms = jnp.mean(x * x, axis=-1, keepdims=True)
    return x * lax.rsqrt(ms + EPS) * g


def _t5_bucket_np(dist):
    n = np.maximum(dist, 0).astype(np.int64)
    max_exact = REL_BUCKETS // 2
    nf = np.maximum(n, 1).astype(np.float64)
    large = max_exact + np.floor(
        np.log(nf / max_exact) / math.log(REL_MAX_DIST / max_exact) * (REL_BUCKETS - max_exact) + 1e-9
    ).astype(np.int64)
    large = np.minimum(large, REL_BUCKETS - 1)
    return np.where(n < max_exact, n, large).astype(np.int32)


def _qkv_seq_kernel(x_ref, g_ref, wq_ref, wkt_ref, wvt_ref, q_ref, kt_ref, vt_ref):
    xn = _rmsnorm(x_ref[...], g_ref[...]).astype(BF16)
    d = D_MODEL
    chunk = 512
    for c in range(0, d, chunk):
        q_ref[:, c:c + chunk] = (_dot(xn, wq_ref[:, c:c + chunk]) * (HEAD_DIM ** -0.5)).astype(q_ref.dtype)
    for c in range(0, d, chunk):
        kt_ref[0, c:c + chunk, :] = _dot_nt(wkt_ref[c:c + chunk, :], xn)
    for c in range(0, d, chunk):
        vt_ref[0, c:c + chunk, :] = _dot_nt(wvt_ref[c:c + chunk, :], xn)


def _qkv_seq(x, g, wq_bf, wkt_bf, wvt_bf, nseq, seq_len, tm):
    m, d = x.shape
    tps = seq_len // tm
    const = lambda i: (0, 0)
    return pl.pallas_call(
        _qkv_seq_kernel,
        grid=(m // tm,),
        in_specs=[
            pl.BlockSpec((tm, d), lambda i: (i, 0)),
            pl.BlockSpec((1, d), const),
            pl.BlockSpec((d, d), const),
            pl.BlockSpec((d, d), const),
            pl.BlockSpec((d, d), const),
        ],
        out_specs=[
            pl.BlockSpec((tm, d), lambda i: (i, 0)),
            pl.BlockSpec((1, d, tm), lambda i: (i // tps, 0, i % tps)),
            pl.BlockSpec((1, d, tm), lambda i: (i // tps, 0, i % tps)),
        ],
        out_shape=[
            jax.ShapeDtypeStruct((m, d), BF16),
            jax.ShapeDtypeStruct((nseq, d, seq_len), F32),
            jax.ShapeDtypeStruct((nseq, d, seq_len), F32),
        ],
        compiler_params=_cparams(("arbitrary",)),
        name="qkv_seq",
    )(x, g, wq_bf, wkt_bf, wvt_bf)


def _mlstm_in_kernel(x_ref, g_ref, w_ref, wg_ref, bg_ref, qk_ref, v_ref, o_ref, gt_ref, *, tm, pad_rows):
    xn = _rmsnorm(x_ref[...], g_ref[...]).astype(BF16)
    hk = M_HEADS * M_QK_DIM
    hv = M_HEADS * M_V_DIM
    qk_ref[:, 0:hk] = (_dot(xn, w_ref[:, 0:hk]) * (M_QK_DIM ** -0.5)).astype(qk_ref.dtype)
    qk_ref[:, hk:2 * hk] = _dot(xn, w_ref[:, hk:2 * hk]).astype(qk_ref.dtype)
    for c in range(0, hv, 512):
        v_ref[:, c:c + 512] = _dot(xn, w_ref[:, 2 * hk + c:2 * hk + c + 512]).astype(v_ref.dtype)
        o_ref[:, c:c + 512] = _dot(xn, w_ref[:, 2 * hk + hv + c:2 * hk + hv + c + 512])
    gates = _dot(xn, wg_ref[...])
    if pad_rows:
        gates = jnp.concatenate([gates, jnp.zeros((pad_rows, LANES), F32)], axis=0)
    gt = gates.T
    gt_ref[...] = gt[0:8, 0:tm] + bg_ref[...]


def _mlstm_in(x, g, w_bf, wg_bf, bg_col, tm, act_dtype):
    m, d = x.shape
    pad_rows = (-tm) % LANES
    kern = functools.partial(_mlstm_in_kernel, tm=tm, pad_rows=pad_rows)
    return pl.pallas_call(
        kern,
        grid=(m // tm,),
        in_specs=[
            pl.BlockSpec((tm, d), lambda i: (i, 0)),
            pl.BlockSpec((1, d), lambda i: (0, 0)),
            pl.BlockSpec(w_bf.shape, lambda i: (0, 0)),
            pl.BlockSpec(wg_bf.shape, lambda i: (0, 0)),
            pl.BlockSpec((8, 1), lambda i: (0, 0)),
        ],
        out_specs=[
            pl.BlockSpec((tm, 1024), lambda i: (i, 0)),
            pl.BlockSpec((tm, 1024), lambda i: (i, 0)),
            pl.BlockSpec((tm, 1024), lambda i: (i, 0)),
            pl.BlockSpec((8, tm), lambda i: (0, i)),
        ],
        out_shape=[
            jax.ShapeDtypeStruct((m, 1024), act_dtype),
            jax.ShapeDtypeStruct((m, 1024), act_dtype),
            jax.ShapeDtypeStruct((m, 1024), F32),
            jax.ShapeDtypeStruct((8, m), F32),
        ],
        compiler_params=_cparams(("arbitrary",)),
        name="mlstm_in",
    )(x, g, w_bf, wg_bf, bg_col)


def _bias_table_kernel(rb_ref, bk_ref, o_ref, *, bucket_sets):
    h = pl.program_id(0)
    c = bk_ref.shape[2]
    for t, buckets in enumerate(bucket_sets):
        bk = bk_ref[t]
        acc = jnp.zeros(bk.shape, F32)
        for bucket in buckets:
            acc = jnp.where(bk == bucket, rb_ref[bucket, h], acc)
        o_ref[0, :, t * c:(t + 1) * c] = acc


def _bias_tables(rel_bias, bucket_np, name):
    nt, r, c = bucket_np.shape
    bucket_sets = tuple(tuple(int(v) for v in np.unique(bucket_np[t])) for t in range(nt))
    kern = functools.partial(_bias_table_kernel, bucket_sets=bucket_sets)
    return pl.pallas_call(
        kern,
        grid=(N_HEADS,),
        in_specs=[
            pl.BlockSpec(memory_space=pltpu.SMEM),
            pl.BlockSpec((nt, r, c), lambda h: (0, 0, 0)),
        ],
        out_specs=pl.BlockSpec((1, r, nt * c), lambda h: (h, 0, 0)),
        out_shape=jax.ShapeDtypeStruct((N_HEADS, r, nt * c), F32),
        compiler_params=_cparams(("arbitrary",)),
        name=name,
    )(rel_bias, jnp.asarray(bucket_np))


def _moba_prompt_kernel(q_ref, kt_ref, vt_ref, bias_ref, o_ref, kaug, vbf, *, seq, nb):
    blk = MOBA_BLOCK
    ktf = kt_ref[0]
    kt_hi = ktf.astype(BF16)
    kt_lo = (ktf - kt_hi.astype(F32)).astype(BF16)
    col_blk = lax.broadcasted_iota(jnp.int32, (LANES, seq), 1) // blk
    row_s = lax.broadcasted_iota(jnp.int32, (LANES, seq), 0)
    kaug[0:LANES, :] = kt_hi
    kaug[LANES:2 * LANES, :] = jnp.where(col_blk == row_s, 1.0, 0.0).astype(BF16)
    vbf[...] = vt_ref[0].astype(BF16)
    mean_w = jnp.where(lax.broadcasted_iota(jnp.int32, (16, seq), 1) // blk
                       == lax.broadcasted_iota(jnp.int32, (16, seq), 0), 1.0 / blk, 0.0).astype(BF16)
    km = _dot_nt(mean_w, kt_hi) + _dot_nt(mean_w, kt_lo)
    km_hi = km.astype(BF16)
    km_lo = (km - km_hi.astype(F32)).astype(BF16)

    lane_q = lax.broadcasted_iota(jnp.int32, (blk, LANES), 1)
    head_lanes = [(lane_q // HEAD_DIM) == h for h in range(2)]
    cand = lax.broadcasted_iota(jnp.int32, (16, blk), 0)
    eye = (lax.broadcasted_iota(jnp.int32, (blk, blk), 0)
           == lax.broadcasted_iota(jnp.int32, (blk, blk), 1)).astype(BF16)
    causal = (lax.broadcasted_iota(jnp.int32, (blk, blk), 1)
              <= lax.broadcasted_iota(jnp.int32, (blk, blk), 0))

    for ob in range(nb):
        q0 = ob * blk
        width = (ob + 1) * blk
        qb = q_ref[0, q0:q0 + blk, :]
        outs = []
        for h in range(2):
            qm = jnp.where(head_lanes[h], qb, jnp.zeros_like(qb))
            if ob <= MOBA_TOPK:
                s = _dot(qm, kaug[0:LANES, 0:width])
            else:
                sc = _dot_nt(km_hi, qm) + _dot_nt(km_lo, qm)
                cnt = jnp.zeros((16, blk), F32)
                for m in range(ob):
                    row = sc[m:m + 1, :]
                    cnt = cnt + jnp.where(row > sc, 1.0, jnp.where((row == sc) & (cand > m), 1.0, 0.0))
                pen_t = jnp.where((cnt < MOBA_TOPK) | (cand >= ob), 0.0, NEG_INF).astype(BF16)
                pen_t = jnp.concatenate([pen_t, jnp.zeros((LANES - 16, blk), BF16)], axis=0)
                pen = _dot_nt(eye, pen_t).astype(BF16)
                s = _dot(jnp.concatenate([qm, pen], axis=1), kaug[:, 0:width])
            s = s + bias_ref[h, :, (nb - 1 - ob) * blk:]
            own = jnp.where(causal, s[:, width - blk:], NEG_INF)
            s = own if ob == 0 else jnp.concatenate([s[:, :width - blk], own], axis=1)
            m_i = jnp.max(s, axis=-1, keepdims=True)
            p = jnp.exp(s - m_i)
            l_i = jnp.sum(p, axis=-1, keepdims=True)
            outs.append(_dot_nt(p.astype(BF16), vbf[:, 0:width]) / l_i)
        o_ref[0, q0:q0 + blk, :] = jnp.where(head_lanes[0], outs[0], outs[1]).astype(o_ref.dtype)


def _moba_prompt(q, kt, vt, bias_strip):
    b, s, d = q.shape
    nb = s // MOBA_BLOCK
    hp = d // LANES
    kern = functools.partial(_moba_prompt_kernel, seq=s, nb=nb)
    return pl.pallas_call(
        kern,
        grid=(hp, b),
        in_specs=[
            pl.BlockSpec((1, s, LANES), lambda p, i: (i, 0, p)),
            pl.BlockSpec((1, LANES, s), lambda p, i: (i, p, 0)),
            pl.BlockSpec((1, LANES, s), lambda p, i: (i, p, 0)),
            pl.BlockSpec((2, MOBA_BLOCK, s), lambda p, i: (p, 0, 0)),
        ],
        out_specs=pl.BlockSpec((1, s, LANES), lambda p, i: (i, 0, p)),
        out_shape=jax.ShapeDtypeStruct((b, s, d), BF16),
        scratch_shapes=[pltpu.VMEM((2 * LANES, s), BF16), pltpu.VMEM((LANES, s), BF16)],
        compiler_params=_cparams(("arbitrary", "arbitrary")),
        name="moba_prompt",
    )(q, kt, vt, bias_strip)


def _ffn_kernel(*refs, tm, tiles_per_seq, seq_mode, final):
    if seq_mode:
        (x_ref, a_ref, wo_ref, g_ref, wup_ref, cw_ref, cb_ref, wdn_ref, gf_ref,
         y_ref, cs_ref, hbuf, carry) = refs
    else:
        (x_ref, a_ref, wo_ref, g_ref, wup_ref, cw_ref, cb_ref, wdn_ref, gf_ref, st0_ref, st1_ref,
         y_ref, cs0_ref, cs1_ref, hbuf) = refs
    i = pl.program_id(0)
    x1 = x_ref[...] + _dot(a_ref[...], wo_ref[...])
    xn = _rmsnorm(x1, g_ref[...]).astype(BF16)
    if seq_mode:
        @pl.when(i % tiles_per_seq == 0)
        def _():
            carry[...] = jnp.zeros_like(carry)
        row = lax.broadcasted_iota(jnp.int32, (tm, 256), 0)
    cf = 256
    for c in range(0, D_FF, cf):
        ug = _dot(xn, wup_ref[:, c:c + cf])
        uv = _dot(xn, wup_ref[:, D_FF + c:D_FF + c + cf])
        if seq_mode:
            c0 = carry[0:1, c:c + cf]
            c1 = carry[1:2, c:c + cf]
            p1 = jnp.where(row == 0, c1, pltpu.roll(ug, 1, axis=0))
            p2 = jnp.where(row == 0, c0, jnp.where(row == 1, c1, pltpu.roll(ug, 2, axis=0)))
            last2 = ug[tm - 2:tm, :]
            carry[0:2, c:c + cf] = last2
            cs_ref[0, :, c:c + cf] = last2
        else:
            p2 = st0_ref[:, c:c + cf]
            p1 = st1_ref[:, c:c + cf]
            cs0_ref[:, c:c + cf] = p1
            cs1_ref[:, c:c + cf] = ug
        conv = (cb_ref[:, c:c + cf] + cw_ref[0:1, c:c + cf] * p2 + cw_ref[1:2, c:c + cf] * p1
                + cw_ref[2:3, c:c + cf] * ug)
        hbuf[:, c:c + cf] = (conv * (1.0 / (1.0 + jnp.exp(-conv))) * uv).astype(hbuf.dtype)
    y = x1 + _dot(hbuf[...], wdn_ref[...])
    if final:
        y = _rmsnorm(y, gf_ref[...])
    y_ref[...] = y


def _ffn(x, a, wo_bf, g, wup_bf, cw, cb, wdn_bf, gf, *, tm, seq_len, state=None, final=False):
    m, d = x.shape
    seq_mode = state is None
    tiles_per_seq = (seq_len // tm) if seq_mode else 1
    kern = functools.partial(_ffn_kernel, tm=tm, tiles_per_seq=tiles_per_seq, seq_mode=seq_mode, final=final)
    const = lambda i: (0, 0)
    resident = functools.partial(pl.BlockSpec, pipeline_mode=pl.Buffered(1))
    in_specs = [
        pl.BlockSpec((tm, d), lambda i: (i, 0)),
        pl.BlockSpec((tm, d), lambda i: (i, 0)),
        resident(wo_bf.shape, const),
        pl.BlockSpec((1, d), const),
        resident(wup_bf.shape, const),
        pl.BlockSpec((3, D_FF), const),
        pl.BlockSpec((1, D_FF), const),
        resident(wdn_bf.shape, const),
        pl.BlockSpec((1, d), const),
    ]
    args = [x, a, wo_bf, g, wup_bf, cw, cb, wdn_bf, gf]
    scratch = [pltpu.VMEM((tm, D_FF), BF16)]
    if seq_mode:
        nseq = m // seq_len
        cs_shapes = [jax.ShapeDtypeStruct((nseq, 2, D_FF), F32)]
        cs_specs = [pl.BlockSpec((1, 2, D_FF), lambda i: (i // tiles_per_seq, 0, 0))]
        scratch.append(pltpu.VMEM((8, D_FF), F32))
    else:
        in_specs += [pl.BlockSpec((tm, D_FF), lambda i: (i, 0))] * 2
        args += [state[:, 0], state[:, 1]]
        cs_shapes = [jax.ShapeDtypeStruct((m, D_FF), F32)] * 2
        cs_specs = [pl.BlockSpec((tm, D_FF), lambda i: (i, 0))] * 2
    outs = pl.pallas_call(
        kern,
        grid=(m // tm,),
        in_specs=in_specs,
        out_specs=[pl.BlockSpec((tm, d), lambda i: (i, 0))] + cs_specs,
        out_shape=[jax.ShapeDtypeStruct((m, d), F32)] + cs_shapes,
        scratch_shapes=scratch,
        compiler_params=_cparams(("arbitrary",)),
        name="ffn_seq" if seq_mode else "ffn_step",
    )(*args)
    if seq_mode:
        return outs[0], outs[1]
    return outs[0], jnp.stack([outs[1], outs[2]], axis=1)


def _log_sigmoid(x):
    return jnp.minimum(x, 0.0) - jnp.log1p(jnp.exp(-jnp.abs(x)))


def _mlstm_seq_kernel(q_ref, k_ref, v_ref, o_ref, gt_ref, gain_ref,
                      h_ref, c_out, n_out, m_out, gate_s, *, seq):
    L = MLSTM_L
    hd = pl.program_id(1)
    tri = (lax.broadcasted_iota(jnp.int32, (L, L), 1) <= lax.broadcasted_iota(jnp.int32, (L, L), 0))
    diag = (lax.broadcasted_iota(jnp.int32, (L, L), 1) == lax.broadcasted_iota(jnp.int32, (L, L), 0))

    ig_all = gt_ref[pl.ds(hd, 1), :]
    lf_all = _log_sigmoid(gt_ref[pl.ds(M_HEADS + hd, 1), :])
    pos = lax.broadcasted_iota(jnp.int32, (1, seq), 1) % L
    bc_all = lf_all
    sh = 1
    while sh < L:
        bc_all = bc_all + jnp.where(pos >= sh, pltpu.roll(bc_all, sh, axis=1), 0.0)
        sh *= 2
    gate_s[0:1, :] = ig_all
    gate_s[1:2, :] = lf_all
    gate_s[2:3, :] = ig_all - bc_all

    c_st = jnp.zeros((M_QK_DIM, M_V_DIM), F32)
    n_st = jnp.zeros((1, M_QK_DIM), F32)
    m_prev = jnp.zeros((1, 1), F32)
    for ci in range(seq // L):
        t0 = ci * L
        ig = gate_s[0:1, t0:t0 + L]
        lf = gate_s[1:2, t0:t0 + L]
        u = gate_s[2:3, t0:t0 + L]
        cmax_c = jnp.max(jnp.where(tri, u, -jnp.inf), axis=-1, keepdims=True)
        bc_c = jnp.sum(jnp.where(tri, lf, 0.0), axis=-1, keepdims=True)
        ig_c = jnp.sum(jnp.where(diag, ig, 0.0), axis=-1, keepdims=True)
        big_m = jnp.maximum(m_prev, cmax_c)
        dmat = jnp.exp(jnp.where(tri, u - big_m, -jnp.inf))
        inter = jnp.exp(m_prev - big_m)
        mt_c = bc_c + big_m

        q = q_ref[0, t0:t0 + L, :]
        k = k_ref[0, t0:t0 + L, :]
        v = v_ref[0, t0:t0 + L, :]
        sm = _dot_nt(q, k) * dmat
        num = _dot(sm.astype(BF16), v) + inter * _dot(q, c_st.astype(BF16))
        qn = jnp.sum(q.astype(F32) * n_st, axis=-1, keepdims=True)
        den = jnp.sum(sm, axis=-1, keepdims=True) + inter * qn
        hout = num / jnp.maximum(jnp.abs(den), jnp.exp(-mt_c))

        bc_last = jnp.sum(lf, axis=-1, keepdims=True)
        m_new = bc_last + jnp.maximum(m_prev, jnp.max(u, axis=-1, keepdims=True))
        decay = jnp.exp(bc_last + m_prev - m_new)
        w_r = jnp.exp(u + bc_last - m_new)
        w_c = jnp.exp(ig_c - bc_c + bc_last - m_new)
        vw = (v.astype(F32) * w_c).astype(BF16)
        c_st = decay * c_st + _dot_tn(k, vw)
        w8 = jnp.broadcast_to(w_r, (8, L)).astype(BF16)
        n_st = decay * n_st + _dot(w8, k)[0:1, :]
        m_prev = m_new

        hn = hout * lax.rsqrt(jnp.mean(hout * hout, axis=-1, keepdims=True) + EPS)
        o = o_ref[0, t0:t0 + L, :]
        hn = hn * gain_ref[...] * (1.0 / (1.0 + jnp.exp(-o)))
        h_ref[0, t0:t0 + L, :] = hn.astype(h_ref.dtype)

    c_out[0, 0] = c_st
    n_out[0, 0] = n_st
    m_out[0, 0] = jnp.broadcast_to(m_prev, (1, LANES))


def _mlstm_seq(qk, v, o, gt, gain, b, s):
    kern = functools.partial(_mlstm_seq_kernel, seq=s)
    qk3 = qk.reshape(b, s, 2 * M_HEADS * M_QK_DIM)
    v3 = v.reshape(b, s, M_HEADS * M_V_DIM)
    o3 = o.reshape(b, s, M_HEADS * M_V_DIM)
    return pl.pallas_call(
        kern,
        grid=(b, M_HEADS),
        in_specs=[
            pl.BlockSpec((1, s, M_QK_DIM), lambda i, h: (i, 0, h)),
            pl.BlockSpec((1, s, M_QK_DIM), lambda i, h: (i, 0, M_HEADS + h)),
            pl.BlockSpec((1, s, M_V_DIM), lambda i, h: (i, 0, h)),
            pl.BlockSpec((1, s, M_V_DIM), lambda i, h: (i, 0, h)),
            pl.BlockSpec((8, s), lambda i, h: (0, i)),
            pl.BlockSpec((1, M_V_DIM), lambda i, h: (0, h)),
        ],
        out_specs=[
            pl.BlockSpec((1, s, M_V_DIM), lambda i, h: (i, 0, h)),
            pl.BlockSpec((1, 1, M_QK_DIM, M_V_DIM), lambda i, h: (i, h, 0, 0)),
            pl.BlockSpec((1, 1, 1, M_QK_DIM), lambda i, h: (i, h, 0, 0)),
            pl.BlockSpec((1, 1, 1, LANES), lambda i, h: (i, h, 0, 0)),
        ],
        out_shape=[
            jax.ShapeDtypeStruct((b, s, M_HEADS * M_V_DIM), BF16),
            jax.ShapeDtypeStruct((b, M_HEADS, M_QK_DIM, M_V_DIM), F32),
            jax.ShapeDtypeStruct((b, M_HEADS, 1, M_QK_DIM), F32),
            jax.ShapeDtypeStruct((b, M_HEADS, 1, LANES), F32),
        ],
        scratch_shapes=[pltpu.VMEM((8, s), F32)],
        compiler_params=_cparams(("arbitrary", "arbitrary")),
        name="mlstm_seq",
    )(qk3, qk3, v3, o3, gt, gain)


def _mlstm_step_kernel(qk_ref, v_ref, o_ref, g_ref, gain_ref, c_ref, n_ref, m_ref,
                       h_ref, c_out, n_out, m_out):
    eye = (lax.broadcasted_iota(jnp.int32, (M_QK_DIM, M_QK_DIM), 0)
           == lax.broadcasted_iota(jnp.int32, (M_QK_DIM, M_QK_DIM), 1))
    hk = M_HEADS * M_QK_DIM
    for h in range(M_HEADS):
        q = qk_ref[0, :, h * M_QK_DIM:(h + 1) * M_QK_DIM]
        k = qk_ref[0, :, hk + h * M_QK_DIM:hk + (h + 1) * M_QK_DIM]
        v = v_ref[0, :, h * M_V_DIM:(h + 1) * M_V_DIM]
        o = o_ref[0, :, h * M_V_DIM:(h + 1) * M_V_DIM]
        ig = g_ref[0, :, h:h + 1]
        lf = _log_sigmoid(g_ref[0, :, M_HEADS + h:M_HEADS + h + 1])
        m_prev = m_ref[0, :, h:h + 1]
        c = c_ref[0, h]
        n = n_ref[0, h]
        q_col = jnp.sum(jnp.where(eye, q, 0.0), axis=-1, keepdims=True)
        k_col = jnp.sum(jnp.where(eye, k, 0.0), axis=-1, keepdims=True)
        g = lf + m_prev
        mt = jnp.maximum(g, ig)
        dm = jnp.exp(ig - mt)
        inter = jnp.exp(g - mt)
        sm = jnp.sum(q * k, axis=-1, keepdims=True) * dm
        qc = jnp.sum(q_col * c, axis=0, keepdims=True)
        num = sm * v + inter * qc
        den = sm + inter * jnp.sum(q * n, axis=-1, keepdims=True)
        hout = num / jnp.maximum(jnp.abs(den), jnp.exp(-mt))
        c_out[0, h] = inter * c + dm * (k_col * v)
        n_out[0, h] = inter * n + dm * k
        m_out[0, :, h:h + 1] = mt
        hn = hout * lax.rsqrt(jnp.mean(hout * hout, axis=-1, keepdims=True) + EPS)
        hn = hn * gain_ref[:, h * M_V_DIM:(h + 1) * M_V_DIM] * (1.0 / (1.0 + jnp.exp(-o)))
        h_ref[0, :, h * M_V_DIM:(h + 1) * M_V_DIM] = hn.astype(h_ref.dtype)


def _mlstm_step(qk, v, o, g_rows, gain, c0, n0, m0):
    nb = qk.shape[0]
    hv = M_HEADS * M_V_DIM
    row3 = lambda a: a.reshape(nb, 1, a.shape[-1])
    spec3 = lambda w: pl.BlockSpec((1, 1, w), lambda i: (i, 0, 0))
    return pl.pallas_call(
        _mlstm_step_kernel,
        grid=(nb,),
        in_specs=[
            spec3(1024), spec3(hv), spec3(hv), spec3(8),
            pl.BlockSpec((1, hv), lambda i: (0, 0)),
            pl.BlockSpec((1, M_HEADS, M_QK_DIM, M_V_DIM), lambda i: (i, 0, 0, 0)),
            pl.BlockSpec((1, M_HEADS, 1, M_QK_DIM), lambda i: (i, 0, 0, 0)),
            spec3(M_HEADS),
        ],
        out_specs=[
            spec3(hv),
            pl.BlockSpec((1, M_HEADS, M_QK_DIM, M_V_DIM), lambda i: (i, 0, 0, 0)),
            pl.BlockSpec((1, M_HEADS, 1, M_QK_DIM), lambda i: (i, 0, 0, 0)),
            spec3(M_HEADS),
        ],
        out_shape=[
            jax.ShapeDtypeStruct((nb, 1, hv), BF16),
            jax.ShapeDtypeStruct((nb, M_HEADS, M_QK_DIM, M_V_DIM), F32),
            jax.ShapeDtypeStruct((nb, M_HEADS, 1, M_QK_DIM), F32),
            jax.ShapeDtypeStruct((nb, 1, M_HEADS), F32),
        ],
        compiler_params=_cparams(("arbitrary",)),
        name="mlstm_step",
    )(row3(qk), row3(v), row3(o), row3(g_rows), gain, c0,
      n0.reshape(nb, M_HEADS, 1, M_QK_DIM), row3(m0))


KMEAN_BUFS = 8


def _kmean_kernel(pt_ref, ck_ref, o_ref, buf, sem, *, n_seq, nblk):
    b = pl.program_id(0)
    total = n_seq * nblk
    ppb = MOBA_BLOCK // PAGE_SIZE

    def copies(g):
        slot = g % KMEAN_BUFS
        bb = g // nblk
        n = g % nblk
        return [pltpu.make_async_copy(ck_ref.at[pt_ref[bb, ppb * n + j]], buf.at[slot, j], sem.at[slot, j])
                for j in range(ppb)]

    group = 2
    ahead = KMEAN_BUFS - group

    @pl.when(b == 0)
    def _():
        for g in range(ahead):
            for cp in copies(g):
                cp.start()

    lane = lax.broadcasted_iota(jnp.int32, (N_HEADS * HEAD_DIM, LANES), 1)

    def body(i, acc):
        g0 = b * nblk + group * i
        for u in range(group):
            for cp in copies(g0 + u):
                cp.wait()
        for u in range(group):
            @pl.when(g0 + ahead + u < total)
            def _():
                for cp in copies(g0 + ahead + u):
                    cp.start()
        for u in range(group):
            slot = (g0 + u) % KMEAN_BUFS
            x = buf[slot, 0]
            for j in range(1, ppb):
                x = x + buf[slot, j]
            ssum = jnp.sum(x.reshape(N_HEADS * HEAD_DIM, PAGE_SIZE), axis=-1, keepdims=True)
            acc = jnp.where(lane == group * i + u, ssum, acc)
        return acc

    acc = lax.fori_loop(0, nblk // group, body, jnp.zeros((N_HEADS * HEAD_DIM, LANES), F32))
    o_ref[0] = acc * (1.0 / MOBA_BLOCK)


def _kmean(page_table, ck_t, nblk):
    n_seq = page_table.shape[0]
    kern = functools.partial(_kmean_kernel, n_seq=n_seq, nblk=nblk)
    ppb = MOBA_BLOCK // PAGE_SIZE
    return pl.pallas_call(
        kern,
        grid_spec=pltpu.PrefetchScalarGridSpec(
            num_scalar_prefetch=1,
            grid=(n_seq,),
            in_specs=[pl.BlockSpec(memory_space=pl.ANY)],
            out_specs=pl.BlockSpec((1, N_HEADS * HEAD_DIM, LANES), lambda b, pt: (b, 0, 0)),
            scratch_shapes=[pltpu.VMEM((KMEAN_BUFS, ppb, N_HEADS, HEAD_DIM, PAGE_SIZE), F32),
                            pltpu.SemaphoreType.DMA((KMEAN_BUFS, ppb))],
        ),
        out_shape=jax.ShapeDtypeStruct((n_seq, N_HEADS * HEAD_DIM, LANES), F32),
        compiler_params=_cparams(("arbitrary",)),
        name="kmean",
    )(page_table, ck_t)


def _select_kernel(qt_ref, km_ref, sel_ref, *, nblk):
    b = pl.program_id(0)
    lane_q = lax.broadcasted_iota(jnp.int32, (N_HEADS * HEAD_DIM, LANES), 1)
    q_col = jnp.sum(jnp.where(lane_q == b, qt_ref[...], 0.0), axis=-1, keepdims=True)
    prod = (km_ref[0] * q_col).reshape(N_HEADS, HEAD_DIM, LANES)
    sc = jnp.sum(prod, axis=1)
    lane = lax.broadcasted_iota(jnp.int32, (N_HEADS, LANES), 1)
    lane_f = lane.astype(F32)
    sc = jnp.where(lane < nblk, sc, -jnp.inf)
    out = jnp.zeros((N_HEADS, LANES), jnp.int32)
    for r in range(MOBA_TOPK):
        mx = jnp.max(sc, axis=-1, keepdims=True)
        idx = jnp.min(jnp.where(sc == mx, lane_f, float(LANES)), axis=-1, keepdims=True)
        out = jnp.where(lane == r, idx.astype(jnp.int32), out)
        sc = jnp.where(lane_f == idx, -jnp.inf, sc)
    sel_ref[0] = out


def _select(qt, kmean_t, nblk):
    n_seq = kmean_t.shape[0]
    kern = functools.partial(_select_kernel, nblk=nblk)
    return pl.pallas_call(
        kern,
        grid=(n_seq,),
        in_specs=[
            pl.BlockSpec(qt.shape, lambda b: (0, 0)),
            pl.BlockSpec((1, N_HEADS * HEAD_DIM, LANES), lambda b: (b, 0, 0)),
        ],
        out_specs=pl.BlockSpec((1, N_HEADS, LANES), lambda b: (b, 0, 0)),
        out_shape=jax.ShapeDtypeStruct((n_seq, N_HEADS, LANES), jnp.int32),
        compiler_params=_cparams(("arbitrary",)),
        name="moba_select",
    )(qt, kmean_t)


def _decode_attn_kernel(sel_ref, pt_ref, ck_ref, cv_ref, qt_ref, kt_ref, vt_ref, bias_ref, o_ref,
                        kbuf, vbuf, sem, *, n_pages, n_seq):
    b = pl.program_id(0)
    ppb = MOBA_BLOCK // PAGE_SIZE
    n_t = MOBA_TOPK * ppb
    self_row = n_pages
    slot = b % 2

    def copies(bb, sl, h, t):
        r, j = divmod(t, ppb)
        lp = ppb * sel_ref[bb, h * MOBA_TOPK + r] + j
        page = pt_ref[bb, lp]
        return (pltpu.make_async_copy(ck_ref.at[page, h], kbuf.at[sl, h, t], sem.at[sl, 0, h, t]),
                pltpu.make_async_copy(cv_ref.at[page, h], vbuf.at[sl, h, t], sem.at[sl, 1, h, t]))

    def start_all(bb, sl):
        def per_head(h, carry):
            for t in range(n_t):
                for cp in copies(bb, sl, h, t):
                    cp.start()
            return carry
        lax.fori_loop(0, N_HEADS, per_head, 0)

    @pl.when(b == 0)
    def _():
        o_ref[...] = jnp.zeros_like(o_ref)
        start_all(0, 0)

    @pl.when(b + 1 < n_seq)
    def _():
        start_all(b + 1, 1 - slot)

    def wait_head(h, carry):
        for t in range(n_t):
            for cp in copies(b, slot, h, t):
                cp.wait()
        return carry
    lax.fori_loop(0, N_HEADS, wait_head, 0)

    lane_b = lax.broadcasted_iota(jnp.int32, (N_HEADS * HEAD_DIM, LANES), 1) == b
    pick = lambda ref: jnp.sum(jnp.where(lane_b, ref[...], 0.0), axis=-1, keepdims=True)
    q_col = pick(qt_ref)
    k_col = pick(kt_ref)
    v_col = pick(vt_ref)
    lane_o = lax.broadcasted_iota(jnp.int32, (HEAD_DIM, LANES), 1) == b

    for h in range(N_HEADS):
        rows = slice(h * HEAD_DIM, (h + 1) * HEAD_DIM)
        qc = q_col[rows]
        s_self = (jnp.sum(qc * k_col[rows], axis=0, keepdims=True)
                  + bias_ref[h, self_row:self_row + 1, 0:1])
        s_list = []
        for t in range(n_t):
            r, j = divmod(t, ppb)
            lp = ppb * sel_ref[b, h * MOBA_TOPK + r] + j
            s_t = (jnp.sum(kbuf[slot, h, t] * qc, axis=0, keepdims=True)
                   + bias_ref[h, pl.ds(lp, 1), :])
            s_list.append(s_t)
        m = s_self
        for s_t in s_list:
            m = jnp.maximum(m, jnp.max(s_t, axis=-1, keepdims=True))
        p_self = jnp.exp(s_self - m)
        l = p_self
        acc = jnp.zeros((HEAD_DIM, LANES), F32)
        for t, s_t in enumerate(s_list):
            p = jnp.exp(s_t - m)
            l = l + jnp.sum(p, axis=-1, keepdims=True)
            acc = acc + vbuf[slot, h, t] * p
        o_col = (jnp.sum(acc, axis=-1, keepdims=True) + p_self * v_col[rows]) / l
        o_ref[rows, :] = jnp.where(lane_o, o_col, o_ref[rows, :])


def _decode_attn(sel, page_table, ck_t, cv_t, qt, kt, vt, bias_s):
    n_seq, n_pages = page_table.shape
    kern = functools.partial(_decode_attn_kernel, n_pages=n_pages, n_seq=n_seq)
    n_t = MOBA_TOPK * (MOBA_BLOCK // PAGE_SIZE)
    full = lambda a: pl.BlockSpec(a.shape, lambda b, s, p: (0,) * a.ndim)
    return pl.pallas_call(
        kern,
        grid_spec=pltpu.PrefetchScalarGridSpec(
            num_scalar_prefetch=2,
            grid=(n_seq,),
            in_specs=[pl.BlockSpec(memory_space=pl.ANY), pl.BlockSpec(memory_space=pl.ANY),
                      full(qt), full(kt), full(vt), full(bias_s)],
            out_specs=pl.BlockSpec((N_HEADS * HEAD_DIM, LANES), lambda b, s, p: (0, 0)),
            scratch_shapes=[pltpu.VMEM((2, N_HEADS, n_t, HEAD_DIM, PAGE_SIZE), F32),
                            pltpu.VMEM((2, N_HEADS, n_t, HEAD_DIM, PAGE_SIZE), F32),
                            pltpu.SemaphoreType.DMA((2, 2, N_HEADS, n_t))],
        ),
        out_shape=jax.ShapeDtypeStruct((N_HEADS * HEAD_DIM, LANES), F32),
        compiler_params=_cparams(("arbitrary",)),
        name="moba_decode",
    )(sel, page_table, ck_t, cv_t, qt, kt, vt, bias_s)


def _qkv_step_kernel(x_ref, g_ref, w_ref, k_ref, v_ref, qt_ref, kt_ref, vt_ref, *, nb):
    xn = _rmsnorm(x_ref[...], g_ref[...]).astype(BF16)
    d = D_MODEL
    pad = jnp.zeros((LANES - nb, d), F32)
    q = _dot(xn, w_ref[:, 0:d]) * (HEAD_DIM ** -0.5)
    k = _dot(xn, w_ref[:, d:2 * d])
    v = _dot(xn, w_ref[:, 2 * d:3 * d])
    k_ref[...] = k
    v_ref[...] = v
    qt_ref[...] = jnp.concatenate([q, pad], axis=0).T
    kt_ref[...] = jnp.concatenate([k, pad], axis=0).T
    vt_ref[...] = jnp.concatenate([v, pad], axis=0).T


def _qkv_step(x, g, w_bf):
    nb, d = x.shape
    kern = functools.partial(_qkv_step_kernel, nb=nb)
    full = lambda shape: pl.BlockSpec(shape, lambda i: (0,) * len(shape))
    return pl.pallas_call(
        kern,
        grid=(1,),
        in_specs=[full((nb, d)), full((1, d)), full(w_bf.shape)],
        out_specs=[full((nb, d)), full((nb, d)), full((d, LANES)), full((d, LANES)), full((d, LANES))],
        out_shape=[jax.ShapeDtypeStruct((nb, d), F32), jax.ShapeDtypeStruct((nb, d), F32),
                   jax.ShapeDtypeStruct((d, LANES), F32), jax.ShapeDtypeStruct((d, LANES), F32),
                   jax.ShapeDtypeStruct((d, LANES), F32)],
        compiler_params=_cparams(("arbitrary",)),
        name="qkv_step",
    )(x, g, w_bf)


def kernel(x_prompt, x_sample, cache_k, cache_v, state_C, state_n, state_m, state_conv, page_table, rel_bias, attn_norm, w_qkv, w_attn_out, mlstm_norm, w_mlstm_in, b_mlstm_gate, mlstm_head_gain, w_mlstm_out, ffn_norm, w_ffn_up, ffn_conv_w, ffn_conv_b, w_ffn_down, final_norm):
    bp, sp, d = x_prompt.shape
    bs, ts, _ = x_sample.shape
    assert ts == 1 and d == D_MODEL and sp % MOBA_BLOCK == 0 and sp % MLSTM_L == 0
    n_pages = page_table.shape[1]
    assert (n_pages * PAGE_SIZE) % MOBA_BLOCK == 0
    nblk = n_pages * PAGE_SIZE // MOBA_BLOCK
    assert MOBA_TOPK <= nblk <= LANES and nblk % 2 == 0
    mp = bp * sp
    nb_p = sp // MOBA_BLOCK
    assert nb_p <= 16
    hk = M_HEADS * M_QK_DIM
    hv = M_HEADS * M_V_DIM
    row = lambda a: a.reshape(1, -1)

    ii = np.arange(MOBA_BLOCK)
    bk_prompt = np.stack([_t5_bucket_np((nb_p - 1 - t) * MOBA_BLOCK + ii[:, None] - ii[None, :])
                          for t in range(nb_p)])
    past = n_pages * PAGE_SIZE
    kpos = np.arange(n_pages * PAGE_SIZE).reshape(n_pages, PAGE_SIZE)
    bk_step = np.concatenate([_t5_bucket_np(past - kpos), np.zeros((8, PAGE_SIZE), np.int32)])[None]
    bias_p = _bias_tables(rel_bias, bk_prompt, "bias_prompt")
    bias_s = _bias_tables(rel_bias, bk_step, "bias_step")

    xp = x_prompt.reshape(mp, d)
    xs = x_sample.reshape(bs, d)

    wqkv = w_qkv[0].astype(BF16)
    wo0 = w_attn_out[0].astype(BF16)
    q_p, kt_p, vt_p = _qkv_seq(xp, row(attn_norm[0]), wqkv[:, :d], wqkv[:, d:2 * d].T, wqkv[:, 2 * d:].T,
                               bp, sp, tm=512)
    attn_p = _moba_prompt(q_p.reshape(bp, sp, d), kt_p, vt_p, bias_p)

    k_s, v_s, qt_s, kt_s, vt_s = _qkv_step(xs, row(attn_norm[0]), wqkv)
    ck_t = jnp.transpose(cache_k[0], (0, 2, 3, 1))
    cv_t = jnp.transpose(cache_v[0], (0, 2, 3, 1))
    kmean_t = _kmean(page_table, ck_t, nblk)
    sel = _select(qt_s, kmean_t, nblk)[:, :, :MOBA_TOPK].reshape(bs, N_HEADS * MOBA_TOPK)
    attn_s_t = _decode_attn(sel, page_table, ck_t, cv_t, qt_s, kt_s, vt_s, bias_s)
    attn_s = attn_s_t.T[:bs].astype(BF16)

    ffn_w = lambda i: (row(ffn_norm[i]), w_ffn_up[i].astype(BF16), ffn_conv_w[i], row(ffn_conv_b[i]),
                       w_ffn_down[i].astype(BF16))
    g0, wup0, cw0, cb0, wdn0 = ffn_w(0)
    gf = row(final_norm)
    hp1, conv_p0 = _ffn(xp, attn_p.reshape(mp, d), wo0, g0, wup0, cw0, cb0, wdn0, gf, tm=512, seq_len=sp)
    hs1, conv_s0 = _ffn(xs, attn_s, wo0, g0, wup0, cw0, cb0, wdn0, gf, tm=bs, seq_len=1, state=state_conv[0])

    w_in = w_mlstm_in[0]
    w_in_bf = w_in[:, :2 * hk + 2 * hv].astype(BF16)
    wg_bf = jnp.pad(w_in[:, 2 * hk + 2 * hv:], ((0, 0), (0, LANES - 2 * M_HEADS))).astype(BF16)
    bg_col = b_mlstm_gate[0].reshape(2 * M_HEADS, 1)
    wo1 = w_mlstm_out[0].astype(BF16)
    gain = row(mlstm_head_gain[0])

    qk_p, vv_p, oo_p, gt_p = _mlstm_in(hp1, row(mlstm_norm[0]), w_in_bf, wg_bf, bg_col, tm=512, act_dtype=BF16)
    hc_p, c_p, n_p, m_p = _mlstm_seq(qk_p, vv_p, oo_p, gt_p, gain, bp, sp)

    qk_s, vv_s, oo_s, gt_s = _mlstm_in(hs1, row(mlstm_norm[0]), w_in_bf, wg_bf, bg_col, tm=bs, act_dtype=F32)
    hc_s, c_s, n_s, m_s = _mlstm_step(qk_s, vv_s, oo_s, gt_s.T, gain, state_C[0], state_n[0], state_m[0])

    g1, wup1, cw1, cb1, wdn1 = ffn_w(1)
    y_p, conv_p1 = _ffn(hp1, hc_p.reshape(mp, d), wo1, g1, wup1, cw1, cb1, wdn1, gf, tm=512, seq_len=sp,
                        final=True)
    y_s, conv_s1 = _ffn(hs1, hc_s.reshape(bs, d), wo1, g1, wup1, cw1, cb1, wdn1, gf, tm=bs, seq_len=1,
                        state=state_conv[1], final=True)

    kv5 = lambda a, n: a.reshape(1, n, -1, N_HEADS, HEAD_DIM)
    kv5t = lambda a: jnp.transpose(a.reshape(1, bp, N_HEADS, HEAD_DIM, sp), (0, 1, 4, 2, 3))
    return (y_p.reshape(bp, sp, d), y_s.reshape(bs, ts, d),
            kv5t(kt_p), kv5t(vt_p), kv5(k_s, bs), kv5(v_s, bs),
            c_p[None], n_p.reshape(1, bp, M_HEADS, M_QK_DIM), m_p[:, :, 0, 0][None],
            c_s[None], n_s.reshape(1, bs, M_HEADS, M_QK_DIM), m_s.reshape(1, bs, M_HEADS),
            jnp.stack([conv_p0, conv_p1]), jnp.stack([conv_s0, conv_s1]))
```

```python
import functools
import math

import numpy as np
import jax
import jax.numpy as jnp
from jax import lax
from jax.experimental import pallas as pl
from jax.experimental.pallas import tpu as pltpu

F32 = jnp.float32
BF16 = jnp.bfloat16

D_MODEL = 1024
N_HEADS = 16
HEAD_DIM = 64
MOBA_BLOCK = 256
MOBA_TOPK = 3
PAGE_SIZE = 128
REL_BUCKETS = 32
REL_MAX_DIST = 4096
M_HEADS = 4
M_QK_DIM = 128
M_V_DIM = 256
D_FF = 2816
EPS = 1e-6
NEG_INF = -1e30
LANES = 128
MLSTM_L = 256
VMEM_LIMIT = 56 * 1024 * 1024
FFN_VMEM_LIMIT = 62 * 1024 * 1024


def _cparams(sem, vmem=VMEM_LIMIT):
    return pltpu.CompilerParams(dimension_semantics=sem, vmem_limit_bytes=vmem)


def _dot(a, b):
    return jnp.dot(a, b, preferred_element_type=F32)


def _dot_nt(a, b):
    return lax.dot_general(a, b, (((1,), (1,)), ((), ())), preferred_element_type=F32)


def _dot_tn(a, b):
    return lax.dot_general(a, b, (((0,), (0,)), ((), ())), preferred_element_type=F32)


def _rmsnorm(x, g):
    ms = jnp.mean(x * x, axis=-1, keepdims=True)
    return x * lax.rsqrt(ms + EPS) * g


def _t5_bucket_np(dist):
    n = np.maximum(dist, 0).astype(np.int64)
    max_exact = REL_BUCKETS // 2
    nf = np.maximum(n, 1).astype(np.float64)
    large = max_exact + np.floor(
        np.log(nf / max_exact) / math.log(REL_MAX_DIST / max_exact) * (REL_BUCKETS - max_exact) + 1e-9
    ).astype(np.int64)
    large = np.minimum(large, REL_BUCKETS - 1)
    return np.where(n < max_exact, n, large).astype(np.int32)


def _qkv_seq_kernel(x_ref, g_ref, wq_ref, wkt_ref, wvt_ref, q_ref, kt_ref, vt_ref):
    xn = _rmsnorm(x_ref[...], g_ref[...]).astype(BF16)
    d = D_MODEL
    chunk = 512
    for c in range(0, d, chunk):
        q_ref[:, c:c + chunk] = (_dot(xn, wq_ref[:, c:c + chunk]) * (HEAD_DIM ** -0.5)).astype(q_ref.dtype)
    for c in range(0, d, chunk):
        kt_ref[0, c:c + chunk, :] = _dot_nt(wkt_ref[c:c + chunk, :], xn)
    for c in range(0, d, chunk):
        vt_ref[0, c:c + chunk, :] = _dot_nt(wvt_ref[c:c + chunk, :], xn)


def _qkv_seq(x, g, wq_bf, wkt_bf, wvt_bf, nseq, seq_len, tm):
    m, d = x.shape
    tps = seq_len // tm
    const = lambda i: (0, 0)
    return pl.pallas_call(
        _qkv_seq_kernel,
        grid=(m // tm,),
        in_specs=[
            pl.BlockSpec((tm, d), lambda i: (i, 0)),
            pl.BlockSpec((1, d), const),
            pl.BlockSpec((d, d), const),
            pl.BlockSpec((d, d), const),
            pl.BlockSpec((d, d), const),
        ],
        out_specs=[
            pl.BlockSpec((tm, d), lambda i: (i, 0)),
            pl.BlockSpec((1, d, tm), lambda i: (i // tps, 0, i % tps)),
            pl.BlockSpec((1, d, tm), lambda i: (i // tps, 0, i % tps)),
        ],
        out_shape=[
            jax.ShapeDtypeStruct((m, d), BF16),
            jax.ShapeDtypeStruct((nseq, d, seq_len), F32),
            jax.ShapeDtypeStruct((nseq, d, seq_len), F32),
        ],
        compiler_params=_cparams(("arbitrary",)),
        name="qkv_seq",
    )(x, g, wq_bf, wkt_bf, wvt_bf)


def _mlstm_in_kernel(x_ref, g_ref, w_ref, wg_ref, bg_ref, qk_ref, v_ref, o_ref, gt_ref, *, tm, pad_rows):
    xn = _rmsnorm(x_ref[...], g_ref[...]).astype(BF16)
    hk = M_HEADS * M_QK_DIM
    hv = M_HEADS * M_V_DIM
    qk_ref[:, 0:hk] = (_dot(xn, w_ref[:, 0:hk]) * (M_QK_DIM ** -0.5)).astype(qk_ref.dtype)
    qk_ref[:, hk:2 * hk] = _dot(xn, w_ref[:, hk:2 * hk]).astype(qk_ref.dtype)
    for c in range(0, hv, 512):
        v_ref[:, c:c + 512] = _dot(xn, w_ref[:, 2 * hk + c:2 * hk + c + 512]).astype(v_ref.dtype)
        o_ref[:, c:c + 512] = _dot(xn, w_ref[:, 2 * hk + hv + c:2 * hk + hv + c + 512])
    gates = _dot(xn, wg_ref[...])
    if pad_rows:
        gates = jnp.concatenate([gates, jnp.zeros((pad_rows, LANES), F32)], axis=0)
    gt = gates.T
    gt_ref[...] = gt[0:8, 0:tm] + bg_ref[...]


def _mlstm_in(x, g, w_bf, wg_bf, bg_col, tm, act_dtype):
    m, d = x.shape
    pad_rows = (-tm) % LANES
    kern = functools.partial(_mlstm_in_kernel, tm=tm, pad_rows=pad_rows)
    return pl.pallas_call(
        kern,
        grid=(m // tm,),
        in_specs=[
            pl.BlockSpec((tm, d), lambda i: (i, 0)),
            pl.BlockSpec((1, d), lambda i: (0, 0)),
            pl.BlockSpec(w_bf.shape, lambda i: (0, 0)),
            pl.BlockSpec(wg_bf.shape, lambda i: (0, 0)),
            pl.BlockSpec((8, 1), lambda i: (0, 0)),
        ],
        out_specs=[
            pl.BlockSpec((tm, 1024), lambda i: (i, 0)),
            pl.BlockSpec((tm, 1024), lambda i: (i, 0)),
            pl.BlockSpec((tm, 1024), lambda i: (i, 0)),
            pl.BlockSpec((8, tm), lambda i: (0, i)),
        ],
        out_shape=[
            jax.ShapeDtypeStruct((m, 1024), act_dtype),
            jax.ShapeDtypeStruct((m, 1024), act_dtype),
            jax.ShapeDtypeStruct((m, 1024), F32),
            jax.ShapeDtypeStruct((8, m), F32),
        ],
        compiler_params=_cparams(("arbitrary",)),
        name="mlstm_in",
    )(x, g, w_bf, wg_bf, bg_col)


def _bias_table_kernel(rb_ref, bk_ref, o_ref, *, bucket_sets):
    h = pl.program_id(0)
    c = bk_ref.shape[2]
    for t, buckets in enumerate(bucket_sets):
        bk = bk_ref[t]
        acc = jnp.zeros(bk.shape, F32)
        for bucket in buckets:
            acc = jnp.where(bk == bucket, rb_ref[bucket, h], acc)
        o_ref[0, :, t * c:(t + 1) * c] = acc


def _bias_tables(rel_bias, bucket_np, name):
    nt, r, c = bucket_np.shape
    bucket_sets = tuple(tuple(int(v) for v in np.unique(bucket_np[t])) for t in range(nt))
    kern = functools.partial(_bias_table_kernel, bucket_sets=bucket_sets)
    return pl.pallas_call(
        kern,
        grid=(N_HEADS,),
        in_specs=[
            pl.BlockSpec(memory_space=pltpu.SMEM),
            pl.BlockSpec((nt, r, c), lambda h: (0, 0, 0)),
        ],
        out_specs=pl.BlockSpec((1, r, nt * c), lambda h: (h, 0, 0)),
        out_shape=jax.ShapeDtypeStruct((N_HEADS, r, nt * c), F32),
        compiler_params=_cparams(("arbitrary",)),
        name=name,
    )(rel_bias, jnp.asarray(bucket_np))


def _moba_prompt_kernel(q_ref, kt_ref, vt_ref, bias_ref, o_ref, kaug, vbf, s_scr, p_scr, *, seq, nb):
    blk = MOBA_BLOCK
    ktf = kt_ref[0]
    kt_hi = ktf.astype(BF16)
    kt_lo = (ktf - kt_hi.astype(F32)).astype(BF16)
    col_blk = lax.broadcasted_iota(jnp.int32, (LANES, seq), 1) // blk
    row_s = lax.broadcasted_iota(jnp.int32, (LANES, seq), 0)
    kaug[0:LANES, :] = kt_hi
    kaug[LANES:2 * LANES, :] = jnp.where(col_blk == row_s, 1.0, 0.0).astype(BF16)
    vbf[...] = vt_ref[0].astype(BF16)
    mean_w = jnp.where(lax.broadcasted_iota(jnp.int32, (16, seq), 1) // blk
                       == lax.broadcasted_iota(jnp.int32, (16, seq), 0), 1.0 / blk, 0.0).astype(BF16)
    km = _dot_nt(mean_w, kt_hi) + _dot_nt(mean_w, kt_lo)
    km_hi = km.astype(BF16)
    km_lo = (km - km_hi.astype(F32)).astype(BF16)

    lane_q = lax.broadcasted_iota(jnp.int32, (blk, LANES), 1)
    head_lanes = [(lane_q // HEAD_DIM) == h for h in range(2)]
    cand = lax.broadcasted_iota(jnp.int32, (16, blk), 0)
    eye = (lax.broadcasted_iota(jnp.int32, (blk, blk), 0)
           == lax.broadcasted_iota(jnp.int32, (blk, blk), 1)).astype(BF16)
    causal = (lax.broadcasted_iota(jnp.int32, (blk, blk), 1)
              <= lax.broadcasted_iota(jnp.int32, (blk, blk), 0))

    items = [(ob, h) for ob in range(nb) for h in range(2)]
    n_slots = s_scr.shape[0]
    state = {}

    def stage_logits(t):
        ob, h = items[t]
        slot = t % n_slots
        q0 = ob * blk
        qb = q_ref[0, q0:q0 + blk, :]
        qm = jnp.where(head_lanes[h], qb, jnp.zeros_like(qb))
        if ob <= MOBA_TOPK:
            lhs, k_rows = qm, LANES
        else:
            sc = _dot_nt(km_hi, qm) + _dot_nt(km_lo, qm)
            cnt = jnp.zeros((16, blk), F32)
            for m in range(ob):
                row = sc[m:m + 1, :]
                cnt = cnt + jnp.where(row > sc, 1.0, jnp.where((row == sc) & (cand > m), 1.0, 0.0))
            pen_t = jnp.where((cnt < MOBA_TOPK) | (cand >= ob), 0.0, NEG_INF).astype(BF16)
            pen_t = jnp.concatenate([pen_t, jnp.zeros((LANES - 16, blk), BF16)], axis=0)
            pen = _dot_nt(eye, pen_t).astype(BF16)
            lhs, k_rows = jnp.concatenate([qm, pen], axis=1), 2 * LANES
        m_part = None
        for n in range(ob + 1):
            cols = slice(n * blk, (n + 1) * blk)
            s_n = _dot(lhs, kaug[0:k_rows, cols]) + bias_ref[h, :, (nb - 1 - ob + n) * blk:(nb - ob + n) * blk]
            if n == ob:
                s_n = jnp.where(causal, s_n, NEG_INF)
            s_scr[slot, :, cols] = s_n
            t_max = jnp.maximum(s_n[:, 0:LANES], s_n[:, LANES:])
            m_part = t_max if m_part is None else jnp.maximum(m_part, t_max)
            if n < ob:
                yield
        state[t] = {"m": jnp.max(m_part, axis=-1, keepdims=True)}
        yield

    def stage_softmax(t):
        ob, h = items[t]
        slot = t % n_slots
        m_i = state[t]["m"]
        l_part = None
        for n in range(ob + 1):
            cols = slice(n * blk, (n + 1) * blk)
            p_n = jnp.exp(s_scr[slot, :, cols] - m_i)
            p_scr[slot, :, cols] = p_n.astype(p_scr.dtype)
            t_sum = p_n[:, 0:LANES] + p_n[:, LANES:]
            l_part = t_sum if l_part is None else l_part + t_sum
            if n < ob:
                yield
        state[t]["l"] = jnp.sum(l_part, axis=-1, keepdims=True)
        yield

    def stage_pv(t):
        ob, h = items[t]
        slot = t % n_slots
        acc = None
        for n in range(ob + 1):
            cols = slice(n * blk, (n + 1) * blk)
            part = _dot_nt(p_scr[slot, :, cols], vbf[:, cols])
            acc = part if acc is None else acc + part
            if n < ob:
                yield
        out = acc / state[t]["l"]
        if h == 0:
            state[t]["out"] = out
        else:
            q0 = ob * blk
            o_ref[0, q0:q0 + blk, :] = jnp.where(head_lanes[0], state[t - 1]["out"], out).astype(o_ref.dtype)
            del state[t - 1], state[t]
        yield

    for t in range(len(items) + 2):
        gens = []
        if t < len(items):
            gens.append(stage_logits(t))
        if 0 <= t - 1 < len(items):
            gens.append(stage_softmax(t - 1))
        if 0 <= t - 2 < len(items):
            gens.append(stage_pv(t - 2))
        while gens:
            for gen in list(gens):
                if next(gen, "done") == "done":
                    gens.remove(gen)


def _moba_prompt(q, kt, vt, bias_strip):
    b, s, d = q.shape
    nb = s // MOBA_BLOCK
    hp = d // LANES
    kern = functools.partial(_moba_prompt_kernel, seq=s, nb=nb)
    return pl.pallas_call(
        kern,
        grid=(hp, b),
        in_specs=[
            pl.BlockSpec((1, s, LANES), lambda p, i: (i, 0, p)),
            pl.BlockSpec((1, LANES, s), lambda p, i: (i, p, 0)),
            pl.BlockSpec((1, LANES, s), lambda p, i: (i, p, 0)),
            pl.BlockSpec((2, MOBA_BLOCK, s), lambda p, i: (p, 0, 0)),
        ],
        out_specs=pl.BlockSpec((1, s, LANES), lambda p, i: (i, 0, p)),
        out_shape=jax.ShapeDtypeStruct((b, s, d), BF16),
        scratch_shapes=[pltpu.VMEM((2 * LANES, s), BF16), pltpu.VMEM((LANES, s), BF16),
                        pltpu.VMEM((4, MOBA_BLOCK, s), F32), pltpu.VMEM((4, MOBA_BLOCK, s), BF16)],
        compiler_params=_cparams(("arbitrary", "arbitrary")),
        name="moba_prompt",
    )(q, kt, vt, bias_strip)


class _KmPlan:
    def __init__(self, n_steps, n_seq, nblk):
        total = n_seq * nblk
        assert total % n_steps == 0
        self.n_seq, self.nblk, self.total = n_seq, nblk, total
        self.per_step = total // n_steps
        assert nblk % self.per_step == 0
        self.steps_per_seq = nblk // self.per_step
        self.group = 4 if self.per_step % 4 == 0 else (2 if self.per_step % 2 == 0 else 1)
        self.points = self.per_step // self.group
        self.ahead = 2 * self.group
        self.nbuf = self.ahead + self.group


def _ffn_kernel(*refs, tm, tiles_per_seq, seq_mode, final, km=None):
    if km is not None:
        pt_ref, refs = refs[0], refs[1:]
        (x_ref, a_ref, wo_ref, g_ref, wup_ref, cw_ref, cb_ref, wdn_ref, gf_ref, ck_ref,
         y_ref, cs_ref, km_ref, hbuf, carry, kbuf, ksem) = refs
    elif seq_mode:
        (x_ref, a_ref, wo_ref, g_ref, wup_ref, cw_ref, cb_ref, wdn_ref, gf_ref,
         y_ref, cs_ref, hbuf, carry) = refs
    else:
        (x_ref, a_ref, wo_ref, g_ref, wup_ref, cw_ref, cb_ref, wdn_ref, gf_ref, st0_ref, st1_ref,
         y_ref, cs0_ref, cs1_ref, hbuf) = refs
    i = pl.program_id(0)
    cf = 256
    n_chunks = D_FF // cf

    if km is not None:
        ppb = MOBA_BLOCK // PAGE_SIZE
        lane_km = lax.broadcasted_iota(jnp.int32, (N_HEADS * HEAD_DIM, LANES), 1)

        def km_copies(g):
            slot = g % km.nbuf
            sq = g // km.nblk
            n = g % km.nblk
            return [pltpu.make_async_copy(ck_ref.at[pt_ref[sq, ppb * n + j]], kbuf.at[slot, j], ksem.at[slot, j])
                    for j in range(ppb)]

        @pl.when(i == 0)
        def _():
            for g in range(km.ahead):
                for cp in km_copies(g):
                    cp.start()

        @pl.when(i % km.steps_per_seq == 0)
        def _():
            km_ref[...] = jnp.zeros_like(km_ref)

        def km_point(p):
            g0 = i * km.per_step + p * km.group
            for u in range(km.group):
                for cp in km_copies(g0 + u):
                    cp.wait()
            for u in range(km.group):
                @pl.when(g0 + km.ahead + u < km.total)
                def _():
                    for cp in km_copies(g0 + km.ahead + u):
                        cp.start()
            acc = km_ref[0]
            for u in range(km.group):
                slot = (g0 + u) % km.nbuf
                x = kbuf[slot, 0]
                for j in range(1, ppb):
                    x = x + kbuf[slot, j]
                ssum = jnp.sum(x.reshape(N_HEADS * HEAD_DIM, PAGE_SIZE), axis=-1, keepdims=True)
                acc = jnp.where(lane_km == (g0 + u) % km.nblk, ssum, acc)
            km_ref[0] = acc

        km_at = {}
        for p in range(km.points):
            km_at.setdefault((p * n_chunks) // km.points, []).append(p)

    x1 = x_ref[...] + _dot(a_ref[...], wo_ref[...])
    xn = _rmsnorm(x1, g_ref[...]).astype(BF16)
    if seq_mode:
        @pl.when(i % tiles_per_seq == 0)
        def _():
            carry[...] = jnp.zeros_like(carry)
        row = lax.broadcasted_iota(jnp.int32, (tm, 256), 0)
    for ci, c in enumerate(range(0, D_FF, cf)):
        if km is not None:
            for p in km_at.get(ci, ()):
                km_point(p)
        ug = _dot(xn, wup_ref[:, c:c + cf])
        uv = _dot(xn, wup_ref[:, D_FF + c:D_FF + c + cf])
        if seq_mode:
            c0 = carry[0:1, c:c + cf]
            c1 = carry[1:2, c:c + cf]
            p1 = jnp.where(row == 0, c1, pltpu.roll(ug, 1, axis=0))
            p2 = jnp.where(row == 0, c0, jnp.where(row == 1, c1, pltpu.roll(ug, 2, axis=0)))
            last2 = ug[tm - 2:tm, :]
            carry[0:2, c:c + cf] = last2
            cs_ref[0, :, c:c + cf] = last2
        else:
            p2 = st0_ref[:, c:c + cf]
            p1 = st1_ref[:, c:c + cf]
            cs0_ref[:, c:c + cf] = p1
            cs1_ref[:, c:c + cf] = ug
        conv = (cb_ref[:, c:c + cf] + cw_ref[0:1, c:c + cf] * p2 + cw_ref[1:2, c:c + cf] * p1
                + cw_ref[2:3, c:c + cf] * ug)
        hbuf[:, c:c + cf] = (conv * (1.0 / (1.0 + jnp.exp(-conv))) * uv).astype(hbuf.dtype)
    y = x1 + _dot(hbuf[...], wdn_ref[...])
    if final:
        y = _rmsnorm(y, gf_ref[...])
    y_ref[...] = y


def _ffn(x, a, wo_bf, g, wup_bf, cw, cb, wdn_bf, gf, *, tm, seq_len, state=None, final=False, pages=None):
    m, d = x.shape
    seq_mode = state is None
    tiles_per_seq = (seq_len // tm) if seq_mode else 1
    n_steps = m // tm
    km = None
    if pages is not None:
        pt_rows, ck_t, nblk = pages
        km = _KmPlan(n_steps, pt_rows.shape[0], nblk)
    kern = functools.partial(_ffn_kernel, tm=tm, tiles_per_seq=tiles_per_seq, seq_mode=seq_mode, final=final,
                             km=km)
    const = lambda i, *_: (0, 0)
    rows = lambda i, *_: (i, 0)
    resident = functools.partial(pl.BlockSpec, pipeline_mode=pl.Buffered(1))
    in_specs = [
        pl.BlockSpec((tm, d), rows),
        pl.BlockSpec((tm, d), rows),
        resident(wo_bf.shape, const),
        pl.BlockSpec((1, d), const),
        resident(wup_bf.shape, const),
        pl.BlockSpec((3, D_FF), const),
        pl.BlockSpec((1, D_FF), const),
        resident(wdn_bf.shape, const),
        pl.BlockSpec((1, d), const),
    ]
    args = [x, a, wo_bf, g, wup_bf, cw, cb, wdn_bf, gf]
    scratch = [pltpu.VMEM((tm, D_FF), BF16)]
    if seq_mode:
        nseq = m // seq_len
        cs_shapes = [jax.ShapeDtypeStruct((nseq, 2, D_FF), F32)]
        cs_specs = [pl.BlockSpec((1, 2, D_FF), lambda i, *_: (i // tiles_per_seq, 0, 0))]
        scratch.append(pltpu.VMEM((8, D_FF), F32))
    else:
        in_specs += [pl.BlockSpec((tm, D_FF), rows)] * 2
        args += [state[:, 0], state[:, 1]]
        cs_shapes = [jax.ShapeDtypeStruct((m, D_FF), F32)] * 2
        cs_specs = [pl.BlockSpec((tm, D_FF), rows)] * 2
    n_prefetch = 0
    if km is not None:
        assert seq_mode
        ppb = MOBA_BLOCK // PAGE_SIZE
        n_prefetch = 1
        in_specs.append(pl.BlockSpec(memory_space=pl.ANY))
        args = [pt_rows] + args + [ck_t]
        cs_shapes.append(jax.ShapeDtypeStruct((km.n_seq, N_HEADS * HEAD_DIM, LANES), F32))
        cs_specs.append(pl.BlockSpec((1, N_HEADS * HEAD_DIM, LANES),
                                     lambda i, *_: (i // km.steps_per_seq, 0, 0)))
        scratch += [pltpu.VMEM((km.nbuf, ppb, N_HEADS, HEAD_DIM, PAGE_SIZE), F32),
                    pltpu.SemaphoreType.DMA((km.nbuf, ppb))]
    outs = pl.pallas_call(
        kern,
        grid_spec=pltpu.PrefetchScalarGridSpec(
            num_scalar_prefetch=n_prefetch,
            grid=(n_steps,),
            in_specs=in_specs,
            out_specs=[pl.BlockSpec((tm, d), rows)] + cs_specs,
            scratch_shapes=scratch,
        ),
        out_shape=[jax.ShapeDtypeStruct((m, d), F32)] + cs_shapes,
        compiler_params=_cparams(("arbitrary",), vmem=FFN_VMEM_LIMIT if km is not None else VMEM_LIMIT),
        name="ffn_seq" if seq_mode else "ffn_step",
    )(*args)
    if not seq_mode:
        return outs[0], jnp.stack([outs[1], outs[2]], axis=1)
    return tuple(outs)


def _log_sigmoid(x):
    return jnp.minimum(x, 0.0) - jnp.log1p(jnp.exp(-jnp.abs(x)))


def _mlstm_seq_kernel(q_ref, k_ref, v_ref, o_ref, gt_ref, gain_ref,
                      h_ref, c_out, n_out, m_out, gate_s, *, seq):
    L = MLSTM_L
    hd = pl.program_id(1)
    tri = (lax.broadcasted_iota(jnp.int32, (L, L), 1) <= lax.broadcasted_iota(jnp.int32, (L, L), 0))
    diag = (lax.broadcasted_iota(jnp.int32, (L, L), 1) == lax.broadcasted_iota(jnp.int32, (L, L), 0))

    ig_all = gt_ref[pl.ds(hd, 1), :]
    lf_all = _log_sigmoid(gt_ref[pl.ds(M_HEADS + hd, 1), :])
    pos = lax.broadcasted_iota(jnp.int32, (1, seq), 1) % L
    bc_all = lf_all
    sh = 1
    while sh < L:
        bc_all = bc_all + jnp.where(pos >= sh, pltpu.roll(bc_all, sh, axis=1), 0.0)
        sh *= 2
    gate_s[0:1, :] = ig_all
    gate_s[1:2, :] = lf_all
    gate_s[2:3, :] = ig_all - bc_all

    c_st = jnp.zeros((M_QK_DIM, M_V_DIM), F32)
    n_st = jnp.zeros((1, M_QK_DIM), F32)
    m_prev = jnp.zeros((1, 1), F32)
    for ci in range(seq // L):
        t0 = ci * L
        ig = gate_s[0:1, t0:t0 + L]
        lf = gate_s[1:2, t0:t0 + L]
        u = gate_s[2:3, t0:t0 + L]
        cmax_c = jnp.max(jnp.where(tri, u, -jnp.inf), axis=-1, keepdims=True)
        bc_c = jnp.sum(jnp.where(tri, lf, 0.0), axis=-1, keepdims=True)
        ig_c = jnp.sum(jnp.where(diag, ig, 0.0), axis=-1, keepdims=True)
        big_m = jnp.maximum(m_prev, cmax_c)
        dmat = jnp.exp(jnp.where(tri, u - big_m, -jnp.inf))
        inter = jnp.exp(m_prev - big_m)
        mt_c = bc_c + big_m

        q = q_ref[0, t0:t0 + L, :]
        k = k_ref[0, t0:t0 + L, :]
        v = v_ref[0, t0:t0 + L, :]
        sm = _dot_nt(q, k) * dmat
        num = _dot(sm.astype(BF16), v) + inter * _dot(q, c_st.astype(BF16))
        qn = jnp.sum(q.astype(F32) * n_st, axis=-1, keepdims=True)
        den = jnp.sum(sm, axis=-1, keepdims=True) + inter * qn
        hout = num / jnp.maximum(jnp.abs(den), jnp.exp(-mt_c))

        bc_last = jnp.sum(lf, axis=-1, keepdims=True)
        m_new = bc_last + jnp.maximum(m_prev, jnp.max(u, axis=-1, keepdims=True))
        decay = jnp.exp(bc_last + m_prev - m_new)
        w_r = jnp.exp(u + bc_last - m_new)
        w_c = jnp.exp(ig_c - bc_c + bc_last - m_new)
        vw = (v.astype(F32) * w_c).astype(BF16)
        c_st = decay * c_st + _dot_tn(k, vw)
        w8 = jnp.broadcast_to(w_r, (8, L)).astype(BF16)
        n_st = decay * n_st + _dot(w8, k)[0:1, :]
        m_prev = m_new

        hn = hout * lax.rsqrt(jnp.mean(hout * hout, axis=-1, keepdims=True) + EPS)
        o = o_ref[0, t0:t0 + L, :]
        hn = hn * gain_ref[...] * (1.0 / (1.0 + jnp.exp(-o)))
        h_ref[0, t0:t0 + L, :] = hn.astype(h_ref.dtype)

    c_out[0, 0] = c_st
    n_out[0, 0] = n_st
    m_out[0, 0] = jnp.broadcast_to(m_prev, (1, LANES))


def _mlstm_seq(qk, v, o, gt, gain, b, s):
    kern = functools.partial(_mlstm_seq_kernel, seq=s)
    qk3 = qk.reshape(b, s, 2 * M_HEADS * M_QK_DIM)
    v3 = v.reshape(b, s, M_HEADS * M_V_DIM)
    o3 = o.reshape(b, s, M_HEADS * M_V_DIM)
    return pl.pallas_call(
        kern,
        grid=(b, M_HEADS),
        in_specs=[
            pl.BlockSpec((1, s, M_QK_DIM), lambda i, h: (i, 0, h)),
            pl.BlockSpec((1, s, M_QK_DIM), lambda i, h: (i, 0, M_HEADS + h)),
            pl.BlockSpec((1, s, M_V_DIM), lambda i, h: (i, 0, h)),
            pl.BlockSpec((1, s, M_V_DIM), lambda i, h: (i, 0, h)),
            pl.BlockSpec((8, s), lambda i, h: (0, i)),
            pl.BlockSpec((1, M_V_DIM), lambda i, h: (0, h)),
        ],
        out_specs=[
            pl.BlockSpec((1, s, M_V_DIM), lambda i, h: (i, 0, h)),
            pl.BlockSpec((1, 1, M_QK_DIM, M_V_DIM), lambda i, h: (i, h, 0, 0)),
            pl.BlockSpec((1, 1, 1, M_QK_DIM), lambda i, h: (i, h, 0, 0)),
            pl.BlockSpec((1, 1, 1, LANES), lambda i, h: (i, h, 0, 0)),
        ],
        out_shape=[
            jax.ShapeDtypeStruct((b, s, M_HEADS * M_V_DIM), BF16),
            jax.ShapeDtypeStruct((b, M_HEADS, M_QK_DIM, M_V_DIM), F32),
            jax.ShapeDtypeStruct((b, M_HEADS, 1, M_QK_DIM), F32),
            jax.ShapeDtypeStruct((b, M_HEADS, 1, LANES), F32),
        ],
        scratch_shapes=[pltpu.VMEM((8, s), F32)],
        compiler_params=_cparams(("arbitrary", "arbitrary")),
        name="mlstm_seq",
    )(qk3, qk3, v3, o3, gt, gain)


def _mlstm_step_kernel(qk_ref, v_ref, o_ref, g_ref, gain_ref, c_ref, n_ref, m_ref,
                       h_ref, c_out, n_out, m_out):
    eye = (lax.broadcasted_iota(jnp.int32, (M_QK_DIM, M_QK_DIM), 0)
           == lax.broadcasted_iota(jnp.int32, (M_QK_DIM, M_QK_DIM), 1))
    hk = M_HEADS * M_QK_DIM
    for h in range(M_HEADS):
        q = qk_ref[0, :, h * M_QK_DIM:(h + 1) * M_QK_DIM]
        k = qk_ref[0, :, hk + h * M_QK_DIM:hk + (h + 1) * M_QK_DIM]
        v = v_ref[0, :, h * M_V_DIM:(h + 1) * M_V_DIM]
        o = o_ref[0, :, h * M_V_DIM:(h + 1) * M_V_DIM]
        ig = g_ref[0, :, h:h + 1]
        lf = _log_sigmoid(g_ref[0, :, M_HEADS + h:M_HEADS + h + 1])
        m_prev = m_ref[0, :, h:h + 1]
        c = c_ref[0, h]
        n = n_ref[0, h]
        q_col = jnp.sum(jnp.where(eye, q, 0.0), axis=-1, keepdims=True)
        k_col = jnp.sum(jnp.where(eye, k, 0.0), axis=-1, keepdims=True)
        g = lf + m_prev
        mt = jnp.maximum(g, ig)
        dm = jnp.exp(ig - mt)
        inter = jnp.exp(g - mt)
        sm = jnp.sum(q * k, axis=-1, keepdims=True) * dm
        qc = jnp.sum(q_col * c, axis=0, keepdims=True)
        num = sm * v + inter * qc
        den = sm + inter * jnp.sum(q * n, axis=-1, keepdims=True)
        hout = num / jnp.maximum(jnp.abs(den), jnp.exp(-mt))
        c_out[0, h] = inter * c + dm * (k_col * v)
        n_out[0, h] = inter * n + dm * k
        m_out[0, :, h:h + 1] = mt
        hn = hout * lax.rsqrt(jnp.mean(hout * hout, axis=-1, keepdims=True) + EPS)
        hn = hn * gain_ref[:, h * M_V_DIM:(h + 1) * M_V_DIM] * (1.0 / (1.0 + jnp.exp(-o)))
        h_ref[0, :, h * M_V_DIM:(h + 1) * M_V_DIM] = hn.astype(h_ref.dtype)


def _mlstm_step(qk, v, o, g_rows, gain, c0, n0, m0):
    nb = qk.shape[0]
    hv = M_HEADS * M_V_DIM
    row3 = lambda a: a.reshape(nb, 1, a.shape[-1])
    spec3 = lambda w: pl.BlockSpec((1, 1, w), lambda i: (i, 0, 0))
    return pl.pallas_call(
        _mlstm_step_kernel,
        grid=(nb,),
        in_specs=[
            spec3(1024), spec3(hv), spec3(hv), spec3(8),
            pl.BlockSpec((1, hv), lambda i: (0, 0)),
            pl.BlockSpec((1, M_HEADS, M_QK_DIM, M_V_DIM), lambda i: (i, 0, 0, 0)),
            pl.BlockSpec((1, M_HEADS, 1, M_QK_DIM), lambda i: (i, 0, 0, 0)),
            spec3(M_HEADS),
        ],
        out_specs=[
            spec3(hv),
            pl.BlockSpec((1, M_HEADS, M_QK_DIM, M_V_DIM), lambda i: (i, 0, 0, 0)),
            pl.BlockSpec((1, M_HEADS, 1, M_QK_DIM), lambda i: (i, 0, 0, 0)),
            spec3(M_HEADS),
        ],
        out_shape=[
            jax.ShapeDtypeStruct((nb, 1, hv), BF16),
            jax.ShapeDtypeStruct((nb, M_HEADS, M_QK_DIM, M_V_DIM), F32),
            jax.ShapeDtypeStruct((nb, M_HEADS, 1, M_QK_DIM), F32),
            jax.ShapeDtypeStruct((nb, 1, M_HEADS), F32),
        ],
        compiler_params=_cparams(("arbitrary",)),
        name="mlstm_step",
    )(row3(qk), row3(v), row3(o), row3(g_rows), gain, c0,
      n0.reshape(nb, M_HEADS, 1, M_QK_DIM), row3(m0))


def _select_kernel(qt_ref, km_ref, sel_ref, *, nblk):
    b = pl.program_id(0)
    lane_q = lax.broadcasted_iota(jnp.int32, (N_HEADS * HEAD_DIM, LANES), 1)
    q_col = jnp.sum(jnp.where(lane_q == b, qt_ref[...], 0.0), axis=-1, keepdims=True)
    prod = (km_ref[0] * q_col).reshape(N_HEADS, HEAD_DIM, LANES)
    sc = jnp.sum(prod, axis=1) * (1.0 / MOBA_BLOCK)
    lane = lax.broadcasted_iota(jnp.int32, (N_HEADS, LANES), 1)
    lane_f = lane.astype(F32)
    sc = jnp.where(lane < nblk, sc, -jnp.inf)
    out = jnp.zeros((N_HEADS, LANES), jnp.int32)
    for r in range(MOBA_TOPK):
        mx = jnp.max(sc, axis=-1, keepdims=True)
        idx = jnp.min(jnp.where(sc == mx, lane_f, float(LANES)), axis=-1, keepdims=True)
        out = jnp.where(lane == r, idx.astype(jnp.int32), out)
        sc = jnp.where(lane_f == idx, -jnp.inf, sc)
    sel_ref[0] = out


def _select(qt, kmean_t, nblk):
    n_seq = kmean_t.shape[0]
    kern = functools.partial(_select_kernel, nblk=nblk)
    return pl.pallas_call(
        kern,
        grid=(n_seq,),
        in_specs=[
            pl.BlockSpec(qt.shape, lambda b: (0, 0)),
            pl.BlockSpec((1, N_HEADS * HEAD_DIM, LANES), lambda b: (b, 0, 0)),
        ],
        out_specs=pl.BlockSpec((1, N_HEADS, LANES), lambda b: (b, 0, 0)),
        out_shape=jax.ShapeDtypeStruct((n_seq, N_HEADS, LANES), jnp.int32),
        compiler_params=_cparams(("arbitrary",)),
        name="moba_select",
    )(qt, kmean_t)


def _decode_attn_kernel(sel_ref, pt_ref, ck_ref, cv_ref, qt_ref, kt_ref, vt_ref, bias_ref, o_ref,
                        kbuf, vbuf, sem, *, n_pages, n_seq):
    b = pl.program_id(0)
    ppb = MOBA_BLOCK // PAGE_SIZE
    n_t = MOBA_TOPK * ppb
    self_row = n_pages
    slot = b % 2

    def copies(bb, sl, h, t):
        r, j = divmod(t, ppb)
        lp = ppb * sel_ref[bb, h * MOBA_TOPK + r] + j
        page = pt_ref[bb, lp]
        return (pltpu.make_async_copy(ck_ref.at[page, h], kbuf.at[sl, h, t], sem.at[sl, 0, h, t]),
                pltpu.make_async_copy(cv_ref.at[page, h], vbuf.at[sl, h, t], sem.at[sl, 1, h, t]))

    def start_all(bb, sl):
        def per_head(h, carry):
            for t in range(n_t):
                for cp in copies(bb, sl, h, t):
                    cp.start()
            return carry
        lax.fori_loop(0, N_HEADS, per_head, 0)

    @pl.when(b == 0)
    def _():
        o_ref[...] = jnp.zeros_like(o_ref)
        start_all(0, 0)

    @pl.when(b + 1 < n_seq)
    def _():
        start_all(b + 1, 1 - slot)

    def wait_head(h, carry):
        for t in range(n_t):
            for cp in copies(b, slot, h, t):
                cp.wait()
        return carry
    lax.fori_loop(0, N_HEADS, wait_head, 0)

    lane_b = lax.broadcasted_iota(jnp.int32, (N_HEADS * HEAD_DIM, LANES), 1) == b
    pick = lambda ref: jnp.sum(jnp.where(lane_b, ref[...], 0.0), axis=-1, keepdims=True)
    q_col = pick(qt_ref)
    k_col = pick(kt_ref)
    v_col = pick(vt_ref)
    lane_o = lax.broadcasted_iota(jnp.int32, (HEAD_DIM, LANES), 1) == b

    for h in range(N_HEADS):
        rows = slice(h * HEAD_DIM, (h + 1) * HEAD_DIM)
        qc = q_col[rows]
        s_self = (jnp.sum(qc * k_col[rows], axis=0, keepdims=True)
                  + bias_ref[h, self_row:self_row + 1, 0:1])
        s_list = []
        for t in range(n_t):
            r, j = divmod(t, ppb)
            lp = ppb * sel_ref[b, h * MOBA_TOPK + r] + j
            s_t = (jnp.sum(kbuf[slot, h, t] * qc, axis=0, keepdims=True)
                   + bias_ref[h, pl.ds(lp, 1), :])
            s_list.append(s_t)
        m = s_self
        for s_t in s_list:
            m = jnp.maximum(m, jnp.max(s_t, axis=-1, keepdims=True))
        p_self = jnp.exp(s_self - m)
        l = p_self
        acc = jnp.zeros((HEAD_DIM, LANES), F32)
        for t, s_t in enumerate(s_list):
            p = jnp.exp(s_t - m)
            l = l + jnp.sum(p, axis=-1, keepdims=True)
            acc = acc + vbuf[slot, h, t] * p
        o_col = (jnp.sum(acc, axis=-1, keepdims=True) + p_self * v_col[rows]) / l
        o_ref[rows, :] = jnp.where(lane_o, o_col, o_ref[rows, :])


def _decode_attn(sel, page_table, ck_t, cv_t, qt, kt, vt, bias_s):
    n_seq, n_pages = page_table.shape
    kern = functools.partial(_decode_attn_kernel, n_pages=n_pages, n_seq=n_seq)
    n_t = MOBA_TOPK * (MOBA_BLOCK // PAGE_SIZE)
    full = lambda a: pl.BlockSpec(a.shape, lambda b, s, p: (0,) * a.ndim)
    return pl.pallas_call(
        kern,
        grid_spec=pltpu.PrefetchScalarGridSpec(
            num_scalar_prefetch=2,
            grid=(n_seq,),
            in_specs=[pl.BlockSpec(memory_space=pl.ANY), pl.BlockSpec(memory_space=pl.ANY),
                      full(qt), full(kt), full(vt), full(bias_s)],
            out_specs=pl.BlockSpec((N_HEADS * HEAD_DIM, LANES), lambda b, s, p: (0, 0)),
            scratch_shapes=[pltpu.VMEM((2, N_HEADS, n_t, HEAD_DIM, PAGE_SIZE), F32),
                            pltpu.VMEM((2, N_HEADS, n_t, HEAD_DIM, PAGE_SIZE), F32),
                            pltpu.SemaphoreType.DMA((2, 2, N_HEADS, n_t))],
        ),
        out_shape=jax.ShapeDtypeStruct((N_HEADS * HEAD_DIM, LANES), F32),
        compiler_params=_cparams(("arbitrary",)),
        name="moba_decode",
    )(sel, page_table, ck_t, cv_t, qt, kt, vt, bias_s)


def _qkv_step_kernel(x_ref, g_ref, w_ref, k_ref, v_ref, qt_ref, kt_ref, vt_ref, *, nb):
    xn = _rmsnorm(x_ref[...], g_ref[...]).astype(BF16)
    d = D_MODEL
    pad = jnp.zeros((LANES - nb, d), F32)
    q = _dot(xn, w_ref[:, 0:d]) * (HEAD_DIM ** -0.5)
    k = _dot(xn, w_ref[:, d:2 * d])
    v = _dot(xn, w_ref[:, 2 * d:3 * d])
    k_ref[...] = k
    v_ref[...] = v
    qt_ref[...] = jnp.concatenate([q, pad], axis=0).T
    kt_ref[...] = jnp.concatenate([k, pad], axis=0).T
    vt_ref[...] = jnp.concatenate([v, pad], axis=0).T


def _qkv_step(x, g, w_bf):
    nb, d = x.shape
    kern = functools.partial(_qkv_step_kernel, nb=nb)
    full = lambda shape: pl.BlockSpec(shape, lambda i: (0,) * len(shape))
    return pl.pallas_call(
        kern,
        grid=(1,),
        in_specs=[full((nb, d)), full((1, d)), full(w_bf.shape)],
        out_specs=[full((nb, d)), full((nb, d)), full((d, LANES)), full((d, LANES)), full((d, LANES))],
        out_shape=[jax.ShapeDtypeStruct((nb, d), F32), jax.ShapeDtypeStruct((nb, d), F32),
                   jax.ShapeDtypeStruct((d, LANES), F32), jax.ShapeDtypeStruct((d, LANES), F32),
                   jax.ShapeDtypeStruct((d, LANES), F32)],
        compiler_params=_cparams(("arbitrary",)),
        name="qkv_step",
    )(x, g, w_bf)


def kernel(x_prompt, x_sample, cache_k, cache_v, state_C, state_n, state_m, state_conv, page_table, rel_bias, attn_norm, w_qkv, w_attn_out, mlstm_norm, w_mlstm_in, b_mlstm_gate, mlstm_head_gain, w_mlstm_out, ffn_norm, w_ffn_up, ffn_conv_w, ffn_conv_b, w_ffn_down, final_norm):
    bp, sp, d = x_prompt.shape
    bs, ts, _ = x_sample.shape
    assert ts == 1 and d == D_MODEL and sp % MOBA_BLOCK == 0 and sp % MLSTM_L == 0
    n_pages = page_table.shape[1]
    assert (n_pages * PAGE_SIZE) % MOBA_BLOCK == 0
    nblk = n_pages * PAGE_SIZE // MOBA_BLOCK
    assert MOBA_TOPK <= nblk <= LANES and bs % 2 == 0
    mp = bp * sp
    nb_p = sp // MOBA_BLOCK
    assert nb_p <= 16
    hk = M_HEADS * M_QK_DIM
    hv = M_HEADS * M_V_DIM
    row = lambda a: a.reshape(1, -1)

    ii = np.arange(MOBA_BLOCK)
    bk_prompt = np.stack([_t5_bucket_np((nb_p - 1 - t) * MOBA_BLOCK + ii[:, None] - ii[None, :])
                          for t in range(nb_p)])
    past = n_pages * PAGE_SIZE
    kpos = np.arange(n_pages * PAGE_SIZE).reshape(n_pages, PAGE_SIZE)
    bk_step = np.concatenate([_t5_bucket_np(past - kpos), np.zeros((8, PAGE_SIZE), np.int32)])[None]
    bias_p = _bias_tables(rel_bias, bk_prompt, "bias_prompt")
    bias_s = _bias_tables(rel_bias, bk_step, "bias_step")

    xp = x_prompt.reshape(mp, d)
    xs = x_sample.reshape(bs, d)

    wqkv = w_qkv[0].astype(BF16)
    wo0 = w_attn_out[0].astype(BF16)
    q_p, kt_p, vt_p = _qkv_seq(xp, row(attn_norm[0]), wqkv[:, :d], wqkv[:, d:2 * d].T, wqkv[:, 2 * d:].T,
                               bp, sp, tm=512)
    attn_p = _moba_prompt(q_p.reshape(bp, sp, d), kt_p, vt_p, bias_p)

    ck_t = jnp.transpose(cache_k[0], (0, 2, 3, 1))
    cv_t = jnp.transpose(cache_v[0], (0, 2, 3, 1))
    half = bs // 2

    ffn_w = lambda i: (row(ffn_norm[i]), w_ffn_up[i].astype(BF16), ffn_conv_w[i], row(ffn_conv_b[i]),
                       w_ffn_down[i].astype(BF16))
    g0, wup0, cw0, cb0, wdn0 = ffn_w(0)
    gf = row(final_norm)
    hp1, conv_p0, ksum_a = _ffn(xp, attn_p.reshape(mp, d), wo0, g0, wup0, cw0, cb0, wdn0, gf, tm=512, seq_len=sp,
                                pages=(page_table[:half], ck_t, nblk))

    w_in = w_mlstm_in[0]
    w_in_bf = w_in[:, :2 * hk + 2 * hv].astype(BF16)
    wg_bf = jnp.pad(w_in[:, 2 * hk + 2 * hv:], ((0, 0), (0, LANES - 2 * M_HEADS))).astype(BF16)
    bg_col = b_mlstm_gate[0].reshape(2 * M_HEADS, 1)
    wo1 = w_mlstm_out[0].astype(BF16)
    gain = row(mlstm_head_gain[0])

    qk_p, vv_p, oo_p, gt_p = _mlstm_in(hp1, row(mlstm_norm[0]), w_in_bf, wg_bf, bg_col, tm=512, act_dtype=BF16)
    hc_p, c_p, n_p, m_p = _mlstm_seq(qk_p, vv_p, oo_p, gt_p, gain, bp, sp)

    g1, wup1, cw1, cb1, wdn1 = ffn_w(1)
    y_p, conv_p1, ksum_b = _ffn(hp1, hc_p.reshape(mp, d), wo1, g1, wup1, cw1, cb1, wdn1, gf, tm=512, seq_len=sp,
                                final=True, pages=(page_table[half:], ck_t, nblk))

    k_s, v_s, qt_s, kt_s, vt_s = _qkv_step(xs, row(attn_norm[0]), wqkv)
    ksum_t = jnp.concatenate([ksum_a, ksum_b], axis=0)
    sel = _select(qt_s, ksum_t, nblk)[:, :, :MOBA_TOPK].reshape(bs, N_HEADS * MOBA_TOPK)
    attn_s_t = _decode_attn(sel, page_table, ck_t, cv_t, qt_s, kt_s, vt_s, bias_s)
    attn_s = attn_s_t.T[:bs].astype(BF16)
    hs1, conv_s0 = _ffn(xs, attn_s, wo0, g0, wup0, cw0, cb0, wdn0, gf, tm=bs, seq_len=1, state=state_conv[0])

    qk_s, vv_s, oo_s, gt_s = _mlstm_in(hs1, row(mlstm_norm[0]), w_in_bf, wg_bf, bg_col, tm=bs, act_dtype=F32)
    hc_s, c_s, n_s, m_s = _mlstm_step(qk_s, vv_s, oo_s, gt_s.T, gain, state_C[0], state_n[0], state_m[0])
    y_s, conv_s1 = _ffn(hs1, hc_s.reshape(bs, d), wo1, g1, wup1, cw1, cb1, wdn1, gf, tm=bs, seq_len=1,
                        state=state_conv[1], final=True)

    kv5 = lambda a, n: a.reshape(1, n, -1, N_HEADS, HEAD_DIM)
    kv5t = lambda a: jnp.transpose(a.reshape(1, bp, N_HEADS, HEAD_DIM, sp), (0, 1, 4, 2, 3))
    return (y_p.reshape(bp, sp, d), y_s.reshape(bs, ts, d),
            kv5t(kt_p), kv5t(vt_p), kv5(k_s, bs), kv5(v_s, bs),
            c_p[None], n_p.reshape(1, bp, M_HEADS, M_QK_DIM), m_p[:, :, 0, 0][None],
            c_s[None], n_s.reshape(1, bs, M_HEADS, M_QK_DIM), m_s.reshape(1, bs, M_HEADS),
            jnp.stack([conv_p0, conv_p1]), jnp.stack([conv_s0, conv_s1]))
```

```python
import functools
import math

import numpy as np
import jax
import jax.numpy as jnp
from jax import lax
from jax.experimental import pallas as pl
from jax.experimental.pallas import tpu as pltpu

F32 = jnp.float32
BF16 = jnp.bfloat16

D_MODEL = 1024
N_HEADS = 16
HEAD_DIM = 64
MOBA_BLOCK = 256
MOBA_TOPK = 3
PAGE_SIZE = 128
REL_BUCKETS = 32
REL_MAX_DIST = 4096
M_HEADS = 4
M_QK_DIM = 128
M_V_DIM = 256
D_FF = 2816
EPS = 1e-6
NEG_INF = -1e30
LANES = 128
MLSTM_L = 256
VMEM_LIMIT = 56 * 1024 * 1024
FFN_VMEM_LIMIT = 62 * 1024 * 1024


def _cparams(sem, vmem=VMEM_LIMIT):
    return pltpu.CompilerParams(dimension_semantics=sem, vmem_limit_bytes=vmem)


def _dot(a, b):
    return jnp.dot(a, b, preferred_element_type=F32)


def _dot_nt(a, b):
    return lax.dot_general(a, b, (((1,), (1,)), ((), ())), preferred_element_type=F32)


def _dot_tn(a, b):
    return lax.dot_general(a, b, (((0,), (0,)), ((), ())), preferred_element_type=F32)


def _rmsnorm(x, g):
    ms = jnp.mean(x * x, axis=-1, keepdims=True)
    return x * lax.rsqrt(ms + EPS) * g


def _t5_bucket_np(dist):
    n = np.maximum(dist, 0).astype(np.int64)
    max_exact = REL_BUCKETS // 2
    nf = np.maximum(n, 1).astype(np.float64)
    large = max_exact + np.floor(
        np.log(nf / max_exact) / math.log(REL_MAX_DIST / max_exact) * (REL_BUCKETS - max_exact) + 1e-9
    ).astype(np.int64)
    large = np.minimum(large, REL_BUCKETS - 1)
    return np.where(n < max_exact, n, large).astype(np.int32)


def _qkv_seq_kernel(x_ref, g_ref, wq_ref, wkt_ref, wvt_ref, q_ref, kt_ref, vt_ref):
    xn = _rmsnorm(x_ref[...], g_ref[...]).astype(BF16)
    d = D_MODEL
    chunk = 512
    for c in range(0, d, chunk):
        q_ref[:, c:c + chunk] = (_dot(xn, wq_ref[:, c:c + chunk]) * (HEAD_DIM ** -0.5)).astype(q_ref.dtype)
    for c in range(0, d, chunk):
        kt_ref[0, c:c + chunk, :] = _dot_nt(wkt_ref[c:c + chunk, :], xn)
    for c in range(0, d, chunk):
        vt_ref[0, c:c + chunk, :] = _dot_nt(wvt_ref[c:c + chunk, :], xn)


def _qkv_seq(x, g, wq_bf, wkt_bf, wvt_bf, nseq, seq_len, tm):
    m, d = x.shape
    tps = seq_len // tm
    const = lambda i: (0, 0)
    return pl.pallas_call(
        _qkv_seq_kernel,
        grid=(m // tm,),
        in_specs=[
            pl.BlockSpec((tm, d), lambda i: (i, 0)),
            pl.BlockSpec((1, d), const),
            pl.BlockSpec((d, d), const),
            pl.BlockSpec((d, d), const),
            pl.BlockSpec((d, d), const),
        ],
        out_specs=[
            pl.BlockSpec((tm, d), lambda i: (i, 0)),
            pl.BlockSpec((1, d, tm), lambda i: (i // tps, 0, i % tps)),
            pl.BlockSpec((1, d, tm), lambda i: (i // tps, 0, i % tps)),
        ],
        out_shape=[
            jax.ShapeDtypeStruct((m, d), BF16),
            jax.ShapeDtypeStruct((nseq, d, seq_len), F32),
            jax.ShapeDtypeStruct((nseq, d, seq_len), F32),
        ],
        compiler_params=_cparams(("arbitrary",)),
        name="qkv_seq",
    )(x, g, wq_bf, wkt_bf, wvt_bf)


def _mlstm_in_kernel(x_ref, g_ref, w_ref, wg_ref, bg_ref, qk_ref, v_ref, o_ref, gt_ref, *, tm, pad_rows):
    xn = _rmsnorm(x_ref[...], g_ref[...]).astype(BF16)
    hk = M_HEADS * M_QK_DIM
    hv = M_HEADS * M_V_DIM
    qk_ref[:, 0:hk] = (_dot(xn, w_ref[:, 0:hk]) * (M_QK_DIM ** -0.5)).astype(qk_ref.dtype)
    qk_ref[:, hk:2 * hk] = _dot(xn, w_ref[:, hk:2 * hk]).astype(qk_ref.dtype)
    for c in range(0, hv, 512):
        v_ref[:, c:c + 512] = _dot(xn, w_ref[:, 2 * hk + c:2 * hk + c + 512]).astype(v_ref.dtype)
        o_ref[:, c:c + 512] = _dot(xn, w_ref[:, 2 * hk + hv + c:2 * hk + hv + c + 512])
    gates = _dot(xn, wg_ref[...])
    if pad_rows:
        gates = jnp.concatenate([gates, jnp.zeros((pad_rows, LANES), F32)], axis=0)
    gt = gates.T
    gt_ref[...] = gt[0:8, 0:tm] + bg_ref[...]


def _mlstm_in(x, g, w_bf, wg_bf, bg_col, tm, act_dtype):
    m, d = x.shape
    pad_rows = (-tm) % LANES
    kern = functools.partial(_mlstm_in_kernel, tm=tm, pad_rows=pad_rows)
    return pl.pallas_call(
        kern,
        grid=(m // tm,),
        in_specs=[
            pl.BlockSpec((tm, d), lambda i: (i, 0)),
            pl.BlockSpec((1, d), lambda i: (0, 0)),
            pl.BlockSpec(w_bf.shape, lambda i: (0, 0)),
            pl.BlockSpec(wg_bf.shape, lambda i: (0, 0)),
            pl.BlockSpec((8, 1), lambda i: (0, 0)),
        ],
        out_specs=[
            pl.BlockSpec((tm, 1024), lambda i: (i, 0)),
            pl.BlockSpec((tm, 1024), lambda i: (i, 0)),
            pl.BlockSpec((tm, 1024), lambda i: (i, 0)),
            pl.BlockSpec((8, tm), lambda i: (0, i)),
        ],
        out_shape=[
            jax.ShapeDtypeStruct((m, 1024), act_dtype),
            jax.ShapeDtypeStruct((m, 1024), act_dtype),
            jax.ShapeDtypeStruct((m, 1024), F32),
            jax.ShapeDtypeStruct((8, m), F32),
        ],
        compiler_params=_cparams(("arbitrary",)),
        name="mlstm_in",
    )(x, g, w_bf, wg_bf, bg_col)


def _bias_table_kernel(rb_ref, bk_ref, o_ref, *, bucket_sets):
    h = pl.program_id(0)
    c = bk_ref.shape[2]
    for t, buckets in enumerate(bucket_sets):
        bk = bk_ref[t]
        acc = jnp.zeros(bk.shape, F32)
        for bucket in buckets:
            acc = jnp.where(bk == bucket, rb_ref[bucket, h], acc)
        o_ref[0, :, t * c:(t + 1) * c] = acc


def _bias_tables(rel_bias, bucket_np, name):
    nt, r, c = bucket_np.shape
    bucket_sets = tuple(tuple(int(v) for v in np.unique(bucket_np[t])) for t in range(nt))
    kern = functools.partial(_bias_table_kernel, bucket_sets=bucket_sets)
    return pl.pallas_call(
        kern,
        grid=(N_HEADS,),
        in_specs=[
            pl.BlockSpec(memory_space=pltpu.SMEM),
            pl.BlockSpec((nt, r, c), lambda h: (0, 0, 0)),
        ],
        out_specs=pl.BlockSpec((1, r, nt * c), lambda h: (h, 0, 0)),
        out_shape=jax.ShapeDtypeStruct((N_HEADS, r, nt * c), F32),
        compiler_params=_cparams(("arbitrary",)),
        name=name,
    )(rel_bias, jnp.asarray(bucket_np))


def _moba_prompt_kernel(q_ref, kt_ref, vt_ref, bias_ref, o_ref, kaug, vbf, s_scr, p_scr, *, seq, nb):
    blk = MOBA_BLOCK
    ktf = kt_ref[0]
    kt_hi = ktf.astype(BF16)
    kt_lo = (ktf - kt_hi.astype(F32)).astype(BF16)
    col_blk = lax.broadcasted_iota(jnp.int32, (LANES, seq), 1) // blk
    row_s = lax.broadcasted_iota(jnp.int32, (LANES, seq), 0)
    kaug[0:LANES, :] = kt_hi
    kaug[LANES:2 * LANES, :] = jnp.where(col_blk == row_s, 1.0, 0.0).astype(BF16)
    vbf[...] = vt_ref[0].astype(BF16)
    mean_w = jnp.where(lax.broadcasted_iota(jnp.int32, (16, seq), 1) // blk
                       == lax.broadcasted_iota(jnp.int32, (16, seq), 0), 1.0 / blk, 0.0).astype(BF16)
    km = _dot_nt(mean_w, kt_hi) + _dot_nt(mean_w, kt_lo)
    km_hi = km.astype(BF16)
    km_lo = (km - km_hi.astype(F32)).astype(BF16)

    lane_q = lax.broadcasted_iota(jnp.int32, (blk, LANES), 1)
    head_lanes = [(lane_q // HEAD_DIM) == h for h in range(2)]
    cand = lax.broadcasted_iota(jnp.int32, (16, blk), 0)
    eye = (lax.broadcasted_iota(jnp.int32, (blk, blk), 0)
           == lax.broadcasted_iota(jnp.int32, (blk, blk), 1)).astype(BF16)
    causal = (lax.broadcasted_iota(jnp.int32, (blk, blk), 1)
              <= lax.broadcasted_iota(jnp.int32, (blk, blk), 0))

    items = [(ob, h) for ob in range(nb) for h in range(2)]
    n_slots = s_scr.shape[0]
    state = {}

    def stage_logits(t):
        ob, h = items[t]
        slot = t % n_slots
        q0 = ob * blk
        qb = q_ref[0, q0:q0 + blk, :]
        qm = jnp.where(head_lanes[h], qb, jnp.zeros_like(qb))
        if ob <= MOBA_TOPK:
            lhs, k_rows = qm, LANES
        else:
            sc = _dot_nt(km_hi, qm) + _dot_nt(km_lo, qm)
            cnt = jnp.zeros((16, blk), F32)
            for m in range(ob):
                row = sc[m:m + 1, :]
                cnt = cnt + jnp.where(row > sc, 1.0, jnp.where((row == sc) & (cand > m), 1.0, 0.0))
            pen_t = jnp.where((cnt < MOBA_TOPK) | (cand >= ob), 0.0, NEG_INF).astype(BF16)
            pen_t = jnp.concatenate([pen_t, jnp.zeros((LANES - 16, blk), BF16)], axis=0)
            pen = _dot_nt(eye, pen_t).astype(BF16)
            lhs, k_rows = jnp.concatenate([qm, pen], axis=1), 2 * LANES
        m_part = None
        for n in range(ob + 1):
            cols = slice(n * blk, (n + 1) * blk)
            s_n = _dot(lhs, kaug[0:k_rows, cols]) + bias_ref[h, :, (nb - 1 - ob + n) * blk:(nb - ob + n) * blk]
            if n == ob:
                s_n = jnp.where(causal, s_n, NEG_INF)
            s_scr[slot, :, cols] = s_n
            t_max = jnp.maximum(s_n[:, 0:LANES], s_n[:, LANES:])
            m_part = t_max if m_part is None else jnp.maximum(m_part, t_max)
            if n < ob:
                yield
        state[t] = {"m": jnp.max(m_part, axis=-1, keepdims=True)}
        yield

    def stage_softmax(t):
        ob, h = items[t]
        slot = t % n_slots
        m_i = state[t]["m"]
        l_part = None
        for n in range(ob + 1):
            cols = slice(n * blk, (n + 1) * blk)
            p_n = jnp.exp(s_scr[slot, :, cols] - m_i)
            p_scr[slot, :, cols] = p_n.astype(p_scr.dtype)
            t_sum = p_n[:, 0:LANES] + p_n[:, LANES:]
            l_part = t_sum if l_part is None else l_part + t_sum
            if n < ob:
                yield
        state[t]["l"] = jnp.sum(l_part, axis=-1, keepdims=True)
        yield

    def stage_pv(t):
        ob, h = items[t]
        slot = t % n_slots
        acc = None
        for n in range(ob + 1):
            cols = slice(n * blk, (n + 1) * blk)
            part = _dot_nt(vbf[:, cols], p_scr[slot, :, cols])
            acc = part if acc is None else acc + part
            if n < ob:
                yield
        out = acc.T / state[t]["l"]
        if h == 0:
            state[t]["out"] = out
        else:
            q0 = ob * blk
            o_ref[0, q0:q0 + blk, :] = jnp.where(head_lanes[0], state[t - 1]["out"], out).astype(o_ref.dtype)
            del state[t - 1], state[t]
        yield

    for t in range(len(items) + 2):
        gens = []
        if t < len(items):
            gens.append(stage_logits(t))
        if 0 <= t - 1 < len(items):
            gens.append(stage_softmax(t - 1))
        if 0 <= t - 2 < len(items):
            gens.append(stage_pv(t - 2))
        while gens:
            for gen in list(gens):
                if next(gen, "done") == "done":
                    gens.remove(gen)


def _moba_prompt(q, kt, vt, bias_strip):
    b, s, d = q.shape
    nb = s // MOBA_BLOCK
    hp = d // LANES
    kern = functools.partial(_moba_prompt_kernel, seq=s, nb=nb)
    return pl.pallas_call(
        kern,
        grid=(hp, b),
        in_specs=[
            pl.BlockSpec((1, s, LANES), lambda p, i: (i, 0, p)),
            pl.BlockSpec((1, LANES, s), lambda p, i: (i, p, 0)),
            pl.BlockSpec((1, LANES, s), lambda p, i: (i, p, 0)),
            pl.BlockSpec((2, MOBA_BLOCK, s), lambda p, i: (p, 0, 0)),
        ],
        out_specs=pl.BlockSpec((1, s, LANES), lambda p, i: (i, 0, p)),
        out_shape=jax.ShapeDtypeStruct((b, s, d), BF16),
        scratch_shapes=[pltpu.VMEM((2 * LANES, s), BF16), pltpu.VMEM((LANES, s), BF16),
                        pltpu.VMEM((4, MOBA_BLOCK, s), F32), pltpu.VMEM((4, MOBA_BLOCK, s), BF16)],
        compiler_params=_cparams(("arbitrary", "arbitrary")),
        name="moba_prompt",
    )(q, kt, vt, bias_strip)


class _KmPlan:
    def __init__(self, n_steps, n_seq, nblk):
        total = n_seq * nblk
        assert total % n_steps == 0
        self.n_seq, self.nblk, self.total = n_seq, nblk, total
        self.per_step = total // n_steps
        assert nblk % self.per_step == 0
        self.steps_per_seq = nblk // self.per_step
        self.group = next(g for g in (8, 4, 2, 1) if self.per_step % g == 0)
        self.points = self.per_step // self.group
        self.ahead = self.group
        self.nbuf = self.ahead + self.group


def _ffn_kernel(*refs, tm, tiles_per_seq, seq_mode, final, km=None):
    if km is not None:
        pt_ref, refs = refs[0], refs[1:]
        (x_ref, a_ref, wo_ref, g_ref, wup_ref, cw_ref, cb_ref, wdn_ref, gf_ref, ck_ref,
         y_ref, cs_ref, km_ref, hbuf, carry, kbuf, ksem) = refs
    elif seq_mode:
        (x_ref, a_ref, wo_ref, g_ref, wup_ref, cw_ref, cb_ref, wdn_ref, gf_ref,
         y_ref, cs_ref, hbuf, carry) = refs
    else:
        (x_ref, a_ref, wo_ref, g_ref, wup_ref, cw_ref, cb_ref, wdn_ref, gf_ref, st0_ref, st1_ref,
         y_ref, cs0_ref, cs1_ref, hbuf) = refs
    i = pl.program_id(0)
    cf = 256
    n_chunks = D_FF // cf

    if km is not None:
        ppb = MOBA_BLOCK // PAGE_SIZE
        lane_km = lax.broadcasted_iota(jnp.int32, (N_HEADS * HEAD_DIM, LANES), 1)

        def km_copies(g):
            slot = g % km.nbuf
            sq = g // km.nblk
            n = g % km.nblk
            return [pltpu.make_async_copy(ck_ref.at[pt_ref[sq, ppb * n + j]], kbuf.at[slot, j], ksem.at[slot, j])
                    for j in range(ppb)]

        @pl.when(i == 0)
        def _():
            for g in range(km.ahead):
                for cp in km_copies(g):
                    cp.start()

        @pl.when(i % km.steps_per_seq == 0)
        def _():
            km_ref[...] = jnp.zeros_like(km_ref)

        def km_wait(p):
            g0 = i * km.per_step + p * km.group
            for u in range(km.group):
                for cp in km_copies(g0 + u):
                    cp.wait()
            for u in range(km.group):
                @pl.when(g0 + km.ahead + u < km.total)
                def _():
                    for cp in km_copies(g0 + km.ahead + u):
                        cp.start()

        def km_reduce(p):
            g0 = i * km.per_step + p * km.group
            acc = km_ref[0]
            for u in range(km.group):
                slot = (g0 + u) % km.nbuf
                x = kbuf[slot, 0]
                for j in range(1, ppb):
                    x = x + kbuf[slot, j]
                ssum = jnp.sum(x.reshape(N_HEADS * HEAD_DIM, PAGE_SIZE), axis=-1, keepdims=True)
                acc = jnp.where(lane_km == (g0 + u) % km.nblk, ssum, acc)
            km_ref[0] = acc

        km_at = {}
        for p in range(km.points):
            km_at.setdefault((p * n_chunks) // km.points, []).append(p)

    x1 = x_ref[...] + _dot(a_ref[...], wo_ref[...])
    xn = _rmsnorm(x1, g_ref[...]).astype(BF16)
    if seq_mode:
        @pl.when(i % tiles_per_seq == 0)
        def _():
            carry[...] = jnp.zeros_like(carry)
        row = lax.broadcasted_iota(jnp.int32, (tm, 256), 0)
    for ci, c in enumerate(range(0, D_FF, cf)):
        if km is not None:
            for p in km_at.get(ci, ()):
                if p > 0:
                    km_reduce(p - 1)
                km_wait(p)
        ug = _dot(xn, wup_ref[:, c:c + cf])
        uv = _dot(xn, wup_ref[:, D_FF + c:D_FF + c + cf])
        if seq_mode:
            c0 = carry[0:1, c:c + cf]
            c1 = carry[1:2, c:c + cf]
            p1 = jnp.where(row == 0, c1, pltpu.roll(ug, 1, axis=0))
            p2 = jnp.where(row == 0, c0, jnp.where(row == 1, c1, pltpu.roll(ug, 2, axis=0)))
            last2 = ug[tm - 2:tm, :]
            carry[0:2, c:c + cf] = last2
            cs_ref[0, :, c:c + cf] = last2
        else:
            p2 = st0_ref[:, c:c + cf]
            p1 = st1_ref[:, c:c + cf]
            cs0_ref[:, c:c + cf] = p1
            cs1_ref[:, c:c + cf] = ug
        conv = (cb_ref[:, c:c + cf] + cw_ref[0:1, c:c + cf] * p2 + cw_ref[1:2, c:c + cf] * p1
                + cw_ref[2:3, c:c + cf] * ug)
        hbuf[:, c:c + cf] = (conv * (1.0 / (1.0 + jnp.exp(-conv))) * uv).astype(hbuf.dtype)
    if km is not None:
        km_reduce(km.points - 1)
    y = x1 + _dot(hbuf[...], wdn_ref[...])
    if final:
        y = _rmsnorm(y, gf_ref[...])
    y_ref[...] = y


def _ffn(x, a, wo_bf, g, wup_bf, cw, cb, wdn_bf, gf, *, tm, seq_len, state=None, final=False, pages=None):
    m, d = x.shape
    seq_mode = state is None
    tiles_per_seq = (seq_len // tm) if seq_mode else 1
    n_steps = m // tm
    km = None
    if pages is not None:
        pt_rows, ck_t, nblk = pages
        km = _KmPlan(n_steps, pt_rows.shape[0], nblk)
    kern = functools.partial(_ffn_kernel, tm=tm, tiles_per_seq=tiles_per_seq, seq_mode=seq_mode, final=final,
                             km=km)
    const = lambda i, *_: (0, 0)
    rows = lambda i, *_: (i, 0)
    resident = functools.partial(pl.BlockSpec, pipeline_mode=pl.Buffered(1))
    in_specs = [
        pl.BlockSpec((tm, d), rows),
        pl.BlockSpec((tm, d), rows),
        resident(wo_bf.shape, const),
        pl.BlockSpec((1, d), const),
        resident(wup_bf.shape, const),
        pl.BlockSpec((3, D_FF), const),
        pl.BlockSpec((1, D_FF), const),
        resident(wdn_bf.shape, const),
        pl.BlockSpec((1, d), const),
    ]
    args = [x, a, wo_bf, g, wup_bf, cw, cb, wdn_bf, gf]
    scratch = [pltpu.VMEM((tm, D_FF), BF16)]
    if seq_mode:
        nseq = m // seq_len
        cs_shapes = [jax.ShapeDtypeStruct((nseq, 2, D_FF), F32)]
        cs_specs = [pl.BlockSpec((1, 2, D_FF), lambda i, *_: (i // tiles_per_seq, 0, 0))]
        scratch.append(pltpu.VMEM((8, D_FF), F32))
    else:
        in_specs += [pl.BlockSpec((tm, D_FF), rows)] * 2
        args += [state[:, 0], state[:, 1]]
        cs_shapes = [jax.ShapeDtypeStruct((m, D_FF), F32)] * 2
        cs_specs = [pl.BlockSpec((tm, D_FF), rows)] * 2
    n_prefetch = 0
    if km is not None:
        assert seq_mode
        ppb = MOBA_BLOCK // PAGE_SIZE
        n_prefetch = 1
        in_specs.append(pl.BlockSpec(memory_space=pl.ANY))
        args = [pt_rows] + args + [ck_t]
        cs_shapes.append(jax.ShapeDtypeStruct((km.n_seq, N_HEADS * HEAD_DIM, LANES), F32))
        cs_specs.append(pl.BlockSpec((1, N_HEADS * HEAD_DIM, LANES),
                                     lambda i, *_: (i // km.steps_per_seq, 0, 0)))
        scratch += [pltpu.VMEM((km.nbuf, ppb, N_HEADS, HEAD_DIM, PAGE_SIZE), F32),
                    pltpu.SemaphoreType.DMA((km.nbuf, ppb))]
    outs = pl.pallas_call(
        kern,
        grid_spec=pltpu.PrefetchScalarGridSpec(
            num_scalar_prefetch=n_prefetch,
            grid=(n_steps,),
            in_specs=in_specs,
            out_specs=[pl.BlockSpec((tm, d), rows)] + cs_specs,
            scratch_shapes=scratch,
        ),
        out_shape=[jax.ShapeDtypeStruct((m, d), F32)] + cs_shapes,
        compiler_params=_cparams(("arbitrary",), vmem=FFN_VMEM_LIMIT if km is not None else VMEM_LIMIT),
        name="ffn_seq" if seq_mode else "ffn_step",
    )(*args)
    if not seq_mode:
        return outs[0], jnp.stack([outs[1], outs[2]], axis=1)
    return tuple(outs)


def _log_sigmoid(x):
    return jnp.minimum(x, 0.0) - jnp.log1p(jnp.exp(-jnp.abs(x)))


def _mlstm_seq_kernel(q_ref, k_ref, v_ref, o_ref, gt_ref, gain_ref,
                      h_ref, c_out, n_out, m_out, gate_s, *, seq):
    L = MLSTM_L
    hd = pl.program_id(1)
    tri = (lax.broadcasted_iota(jnp.int32, (L, L), 1) <= lax.broadcasted_iota(jnp.int32, (L, L), 0))
    diag = (lax.broadcasted_iota(jnp.int32, (L, L), 1) == lax.broadcasted_iota(jnp.int32, (L, L), 0))

    ig_all = gt_ref[pl.ds(hd, 1), :]
    lf_all = _log_sigmoid(gt_ref[pl.ds(M_HEADS + hd, 1), :])
    pos = lax.broadcasted_iota(jnp.int32, (1, seq), 1) % L
    bc_all = lf_all
    sh = 1
    while sh < L:
        bc_all = bc_all + jnp.where(pos >= sh, pltpu.roll(bc_all, sh, axis=1), 0.0)
        sh *= 2
    gate_s[0:1, :] = ig_all
    gate_s[1:2, :] = lf_all
    gate_s[2:3, :] = ig_all - bc_all

    c_st = jnp.zeros((M_QK_DIM, M_V_DIM), F32)
    n_st = jnp.zeros((1, M_QK_DIM), F32)
    m_prev = jnp.zeros((1, 1), F32)
    for ci in range(seq // L):
        t0 = ci * L
        ig = gate_s[0:1, t0:t0 + L]
        lf = gate_s[1:2, t0:t0 + L]
        u = gate_s[2:3, t0:t0 + L]
        cmax_c = jnp.max(jnp.where(tri, u, -jnp.inf), axis=-1, keepdims=True)
        bc_c = jnp.sum(jnp.where(tri, lf, 0.0), axis=-1, keepdims=True)
        ig_c = jnp.sum(jnp.where(diag, ig, 0.0), axis=-1, keepdims=True)
        big_m = jnp.maximum(m_prev, cmax_c)
        dmat = jnp.exp(jnp.where(tri, u - big_m, -jnp.inf))
        inter = jnp.exp(m_prev - big_m)
        mt_c = bc_c + big_m

        q = q_ref[0, t0:t0 + L, :]
        k = k_ref[0, t0:t0 + L, :]
        v = v_ref[0, t0:t0 + L, :]
        sm = _dot_nt(q, k) * dmat
        num = _dot(sm.astype(BF16), v) + inter * _dot(q, c_st.astype(BF16))
        qn = jnp.sum(q.astype(F32) * n_st, axis=-1, keepdims=True)
        den = jnp.sum(sm, axis=-1, keepdims=True) + inter * qn
        hout = num / jnp.maximum(jnp.abs(den), jnp.exp(-mt_c))

        bc_last = jnp.sum(lf, axis=-1, keepdims=True)
        m_new = bc_last + jnp.maximum(m_prev, jnp.max(u, axis=-1, keepdims=True))
        decay = jnp.exp(bc_last + m_prev - m_new)
        w_r = jnp.exp(u + bc_last - m_new)
        w_c = jnp.exp(ig_c - bc_c + bc_last - m_new)
        vw = (v.astype(F32) * w_c).astype(BF16)
        c_st = decay * c_st + _dot_tn(k, vw)
        w8 = jnp.broadcast_to(w_r, (8, L)).astype(BF16)
        n_st = decay * n_st + _dot(w8, k)[0:1, :]
        m_prev = m_new

        hn = hout * lax.rsqrt(jnp.mean(hout * hout, axis=-1, keepdims=True) + EPS)
        o = o_ref[0, t0:t0 + L, :]
        hn = hn * gain_ref[...] * (1.0 / (1.0 + jnp.exp(-o)))
        h_ref[0, t0:t0 + L, :] = hn.astype(h_ref.dtype)

    c_out[0, 0] = c_st
    n_out[0, 0] = n_st
    m_out[0, 0] = jnp.broadcast_to(m_prev, (1, LANES))


def _mlstm_seq(qk, v, o, gt, gain, b, s):
    kern = functools.partial(_mlstm_seq_kernel, seq=s)
    qk3 = qk.reshape(b, s, 2 * M_HEADS * M_QK_DIM)
    v3 = v.reshape(b, s, M_HEADS * M_V_DIM)
    o3 = o.reshape(b, s, M_HEADS * M_V_DIM)
    return pl.pallas_call(
        kern,
        grid=(b, M_HEADS),
        in_specs=[
            pl.BlockSpec((1, s, M_QK_DIM), lambda i, h: (i, 0, h)),
            pl.BlockSpec((1, s, M_QK_DIM), lambda i, h: (i, 0, M_HEADS + h)),
            pl.BlockSpec((1, s, M_V_DIM), lambda i, h: (i, 0, h)),
            pl.BlockSpec((1, s, M_V_DIM), lambda i, h: (i, 0, h)),
            pl.BlockSpec((8, s), lambda i, h: (0, i)),
            pl.BlockSpec((1, M_V_DIM), lambda i, h: (0, h)),
        ],
        out_specs=[
            pl.BlockSpec((1, s, M_V_DIM), lambda i, h: (i, 0, h)),
            pl.BlockSpec((1, 1, M_QK_DIM, M_V_DIM), lambda i, h: (i, h, 0, 0)),
            pl.BlockSpec((1, 1, 1, M_QK_DIM), lambda i, h: (i, h, 0, 0)),
            pl.BlockSpec((1, 1, 1, LANES), lambda i, h: (i, h, 0, 0)),
        ],
        out_shape=[
            jax.ShapeDtypeStruct((b, s, M_HEADS * M_V_DIM), BF16),
            jax.ShapeDtypeStruct((b, M_HEADS, M_QK_DIM, M_V_DIM), F32),
            jax.ShapeDtypeStruct((b, M_HEADS, 1, M_QK_DIM), F32),
            jax.ShapeDtypeStruct((b, M_HEADS, 1, LANES), F32),
        ],
        scratch_shapes=[pltpu.VMEM((8, s), F32)],
        compiler_params=_cparams(("arbitrary", "arbitrary")),
        name="mlstm_seq",
    )(qk3, qk3, v3, o3, gt, gain)


def _mlstm_step_kernel(qk_ref, v_ref, o_ref, g_ref, gain_ref, c_ref, n_ref, m_ref,
                       h_ref, c_out, n_out, m_out):
    eye = (lax.broadcasted_iota(jnp.int32, (M_QK_DIM, M_QK_DIM), 0)
           == lax.broadcasted_iota(jnp.int32, (M_QK_DIM, M_QK_DIM), 1))
    hk = M_HEADS * M_QK_DIM
    for h in range(M_HEADS):
        q = qk_ref[0, :, h * M_QK_DIM:(h + 1) * M_QK_DIM]
        k = qk_ref[0, :, hk + h * M_QK_DIM:hk + (h + 1) * M_QK_DIM]
        v = v_ref[0, :, h * M_V_DIM:(h + 1) * M_V_DIM]
        o = o_ref[0, :, h * M_V_DIM:(h + 1) * M_V_DIM]
        ig = g_ref[0, :, h:h + 1]
        lf = _log_sigmoid(g_ref[0, :, M_HEADS + h:M_HEADS + h + 1])
        m_prev = m_ref[0, :, h:h + 1]
        c = c_ref[0, h]
        n = n_ref[0, h]
        q_col = jnp.sum(jnp.where(eye, q, 0.0), axis=-1, keepdims=True)
        k_col = jnp.sum(jnp.where(eye, k, 0.0), axis=-1, keepdims=True)
        g = lf + m_prev
        mt = jnp.maximum(g, ig)
        dm = jnp.exp(ig - mt)
        inter = jnp.exp(g - mt)
        sm = jnp.sum(q * k, axis=-1, keepdims=True) * dm
        qc = jnp.sum(q_col * c, axis=0, keepdims=True)
        num = sm * v + inter * qc
        den = sm + inter * jnp.sum(q * n, axis=-1, keepdims=True)
        hout = num / jnp.maximum(jnp.abs(den), jnp.exp(-mt))
        c_out[0, h] = inter * c + dm * (k_col * v)
        n_out[0, h] = inter * n + dm * k
        m_out[0, :, h:h + 1] = mt
        hn = hout * lax.rsqrt(jnp.mean(hout * hout, axis=-1, keepdims=True) + EPS)
        hn = hn * gain_ref[:, h * M_V_DIM:(h + 1) * M_V_DIM] * (1.0 / (1.0 + jnp.exp(-o)))
        h_ref[0, :, h * M_V_DIM:(h + 1) * M_V_DIM] = hn.astype(h_ref.dtype)


def _mlstm_step(qk, v, o, g_rows, gain, c0, n0, m0):
    nb = qk.shape[0]
    hv = M_HEADS * M_V_DIM
    row3 = lambda a: a.reshape(nb, 1, a.shape[-1])
    spec3 = lambda w: pl.BlockSpec((1, 1, w), lambda i: (i, 0, 0))
    return pl.pallas_call(
        _mlstm_step_kernel,
        grid=(nb,),
        in_specs=[
            spec3(1024), spec3(hv), spec3(hv), spec3(8),
            pl.BlockSpec((1, hv), lambda i: (0, 0)),
            pl.BlockSpec((1, M_HEADS, M_QK_DIM, M_V_DIM), lambda i: (i, 0, 0, 0)),
            pl.BlockSpec((1, M_HEADS, 1, M_QK_DIM), lambda i: (i, 0, 0, 0)),
            spec3(M_HEADS),
        ],
        out_specs=[
            spec3(hv),
            pl.BlockSpec((1, M_HEADS, M_QK_DIM, M_V_DIM), lambda i: (i, 0, 0, 0)),
            pl.BlockSpec((1, M_HEADS, 1, M_QK_DIM), lambda i: (i, 0, 0, 0)),
            spec3(M_HEADS),
        ],
        out_shape=[
            jax.ShapeDtypeStruct((nb, 1, hv), BF16),
            jax.ShapeDtypeStruct((nb, M_HEADS, M_QK_DIM, M_V_DIM), F32),
            jax.ShapeDtypeStruct((nb, M_HEADS, 1, M_QK_DIM), F32),
            jax.ShapeDtypeStruct((nb, 1, M_HEADS), F32),
        ],
        compiler_params=_cparams(("arbitrary",)),
        name="mlstm_step",
    )(row3(qk), row3(v), row3(o), row3(g_rows), gain, c0,
      n0.reshape(nb, M_HEADS, 1, M_QK_DIM), row3(m0))


def _select_kernel(qt_ref, km_ref, sel_ref, *, nblk, seq_base):
    b = pl.program_id(0) + seq_base
    lane_q = lax.broadcasted_iota(jnp.int32, (N_HEADS * HEAD_DIM, LANES), 1)
    q_col = jnp.sum(jnp.where(lane_q == b, qt_ref[...], 0.0), axis=-1, keepdims=True)
    prod = (km_ref[0] * q_col).reshape(N_HEADS, HEAD_DIM, LANES)
    sc = jnp.sum(prod, axis=1) * (1.0 / MOBA_BLOCK)
    lane = lax.broadcasted_iota(jnp.int32, (N_HEADS, LANES), 1)
    lane_f = lane.astype(F32)
    sc = jnp.where(lane < nblk, sc, -jnp.inf)
    out = jnp.zeros((N_HEADS, LANES), jnp.int32)
    for r in range(MOBA_TOPK):
        mx = jnp.max(sc, axis=-1, keepdims=True)
        idx = jnp.min(jnp.where(sc == mx, lane_f, float(LANES)), axis=-1, keepdims=True)
        out = jnp.where(lane == r, idx.astype(jnp.int32), out)
        sc = jnp.where(lane_f == idx, -jnp.inf, sc)
    sel_ref[0] = out


def _select(qt, kmean_t, nblk, seq_base):
    n_seq = kmean_t.shape[0]
    kern = functools.partial(_select_kernel, nblk=nblk, seq_base=seq_base)
    return pl.pallas_call(
        kern,
        grid=(n_seq,),
        in_specs=[
            pl.BlockSpec(qt.shape, lambda b: (0, 0)),
            pl.BlockSpec((1, N_HEADS * HEAD_DIM, LANES), lambda b: (b, 0, 0)),
        ],
        out_specs=pl.BlockSpec((1, N_HEADS, LANES), lambda b: (b, 0, 0)),
        out_shape=jax.ShapeDtypeStruct((n_seq, N_HEADS, LANES), jnp.int32),
        compiler_params=_cparams(("arbitrary",)),
        name="moba_select",
    )(qt, kmean_t)


def _decode_attn_kernel(sel_ref, pt_ref, ck_ref, cv_ref, qt_ref, kt_ref, vt_ref, bias_ref, o_ref,
                        kbuf, vbuf, sem, *, n_pages, n_seq):
    b = pl.program_id(0)
    ppb = MOBA_BLOCK // PAGE_SIZE
    n_t = MOBA_TOPK * ppb
    self_row = n_pages
    slot = b % 2

    def copies(bb, sl, h, t):
        r, j = divmod(t, ppb)
        lp = ppb * sel_ref[bb, h * MOBA_TOPK + r] + j
        page = pt_ref[bb, lp]
        return (pltpu.make_async_copy(ck_ref.at[page, h], kbuf.at[sl, h, t], sem.at[sl, 0, h, t]),
                pltpu.make_async_copy(cv_ref.at[page, h], vbuf.at[sl, h, t], sem.at[sl, 1, h, t]))

    def start_all(bb, sl):
        def per_head(h, carry):
            for t in range(n_t):
                for cp in copies(bb, sl, h, t):
                    cp.start()
            return carry
        lax.fori_loop(0, N_HEADS, per_head, 0)

    @pl.when(b == 0)
    def _():
        o_ref[...] = jnp.zeros_like(o_ref)
        start_all(0, 0)

    @pl.when(b + 1 < n_seq)
    def _():
        start_all(b + 1, 1 - slot)

    def wait_head(h, carry):
        for t in range(n_t):
            for cp in copies(b, slot, h, t):
                cp.wait()
        return carry
    lax.fori_loop(0, N_HEADS, wait_head, 0)

    lane_b = lax.broadcasted_iota(jnp.int32, (N_HEADS * HEAD_DIM, LANES), 1) == b
    pick = lambda ref: jnp.sum(jnp.where(lane_b, ref[...], 0.0), axis=-1, keepdims=True)
    q_col = pick(qt_ref)
    k_col = pick(kt_ref)
    v_col = pick(vt_ref)
    lane_o = lax.broadcasted_iota(jnp.int32, (HEAD_DIM, LANES), 1) == b

    head_rows = lambda h: slice(h * HEAD_DIM, (h + 1) * HEAD_DIM)
    s_tiles = []
    for t in range(n_t):
        r, j = divmod(t, ppb)
        per_head = []
        for h in range(N_HEADS):
            lp = ppb * sel_ref[b, h * MOBA_TOPK + r] + j
            per_head.append(jnp.sum(kbuf[slot, h, t] * q_col[head_rows(h)], axis=0, keepdims=True)
                            + bias_ref[h, pl.ds(lp, 1), :])
        s_tiles.append(jnp.concatenate(per_head, axis=0))
    s_self = (jnp.sum((q_col * k_col).reshape(N_HEADS, HEAD_DIM, 1), axis=1)
              + bias_ref[:, self_row:self_row + 1, 0:1].reshape(N_HEADS, 1))
    m = s_self
    for s_t in s_tiles:
        m = jnp.maximum(m, jnp.max(s_t, axis=-1, keepdims=True))
    p_self = jnp.exp(s_self - m)
    p_tiles = [jnp.exp(s_t - m) for s_t in s_tiles]
    l = p_self
    for p_t in p_tiles:
        l = l + jnp.sum(p_t, axis=-1, keepdims=True)
    for h in range(N_HEADS):
        acc = vbuf[slot, h, 0] * p_tiles[0][h:h + 1, :]
        for t in range(1, n_t):
            acc = acc + vbuf[slot, h, t] * p_tiles[t][h:h + 1, :]
        o_col = ((jnp.sum(acc, axis=-1, keepdims=True) + p_self[h:h + 1, :] * v_col[head_rows(h)])
                 / l[h:h + 1, :])
        o_ref[head_rows(h), :] = jnp.where(lane_o, o_col, o_ref[head_rows(h), :])


def _decode_attn(sel, page_table, ck_t, cv_t, qt, kt, vt, bias_s):
    n_seq, n_pages = page_table.shape
    kern = functools.partial(_decode_attn_kernel, n_pages=n_pages, n_seq=n_seq)
    n_t = MOBA_TOPK * (MOBA_BLOCK // PAGE_SIZE)
    full = lambda a: pl.BlockSpec(a.shape, lambda b, s, p: (0,) * a.ndim)
    return pl.pallas_call(
        kern,
        grid_spec=pltpu.PrefetchScalarGridSpec(
            num_scalar_prefetch=2,
            grid=(n_seq,),
            in_specs=[pl.BlockSpec(memory_space=pl.ANY), pl.BlockSpec(memory_space=pl.ANY),
                      full(qt), full(kt), full(vt), full(bias_s)],
            out_specs=pl.BlockSpec((N_HEADS * HEAD_DIM, LANES), lambda b, s, p: (0, 0)),
            scratch_shapes=[pltpu.VMEM((2, N_HEADS, n_t, HEAD_DIM, PAGE_SIZE), F32),
                            pltpu.VMEM((2, N_HEADS, n_t, HEAD_DIM, PAGE_SIZE), F32),
                            pltpu.SemaphoreType.DMA((2, 2, N_HEADS, n_t))],
        ),
        out_shape=jax.ShapeDtypeStruct((N_HEADS * HEAD_DIM, LANES), F32),
        compiler_params=_cparams(("arbitrary",)),
        name="moba_decode",
    )(sel, page_table, ck_t, cv_t, qt, kt, vt, bias_s)


def _qkv_step_kernel(x_ref, g_ref, w_ref, k_ref, v_ref, qt_ref, kt_ref, vt_ref, *, nb):
    xn = _rmsnorm(x_ref[...], g_ref[...]).astype(BF16)
    d = D_MODEL
    pad = jnp.zeros((LANES - nb, d), F32)
    q = _dot(xn, w_ref[:, 0:d]) * (HEAD_DIM ** -0.5)
    k = _dot(xn, w_ref[:, d:2 * d])
    v = _dot(xn, w_ref[:, 2 * d:3 * d])
    k_ref[...] = k
    v_ref[...] = v
    qt_ref[...] = jnp.concatenate([q, pad], axis=0).T
    kt_ref[...] = jnp.concatenate([k, pad], axis=0).T
    vt_ref[...] = jnp.concatenate([v, pad], axis=0).T


def _qkv_step(x, g, w_bf):
    nb, d = x.shape
    kern = functools.partial(_qkv_step_kernel, nb=nb)
    full = lambda shape: pl.BlockSpec(shape, lambda i: (0,) * len(shape))
    return pl.pallas_call(
        kern,
        grid=(1,),
        in_specs=[full((nb, d)), full((1, d)), full(w_bf.shape)],
        out_specs=[full((nb, d)), full((nb, d)), full((d, LANES)), full((d, LANES)), full((d, LANES))],
        out_shape=[jax.ShapeDtypeStruct((nb, d), F32), jax.ShapeDtypeStruct((nb, d), F32),
                   jax.ShapeDtypeStruct((d, LANES), F32), jax.ShapeDtypeStruct((d, LANES), F32),
                   jax.ShapeDtypeStruct((d, LANES), F32)],
        compiler_params=_cparams(("arbitrary",)),
        name="qkv_step",
    )(x, g, w_bf)


def kernel(x_prompt, x_sample, cache_k, cache_v, state_C, state_n, state_m, state_conv, page_table, rel_bias, attn_norm, w_qkv, w_attn_out, mlstm_norm, w_mlstm_in, b_mlstm_gate, mlstm_head_gain, w_mlstm_out, ffn_norm, w_ffn_up, ffn_conv_w, ffn_conv_b, w_ffn_down, final_norm):
    bp, sp, d = x_prompt.shape
    bs, ts, _ = x_sample.shape
    assert ts == 1 and d == D_MODEL and sp % MOBA_BLOCK == 0 and sp % MLSTM_L == 0
    n_pages = page_table.shape[1]
    assert (n_pages * PAGE_SIZE) % MOBA_BLOCK == 0
    nblk = n_pages * PAGE_SIZE // MOBA_BLOCK
    assert MOBA_TOPK <= nblk <= LANES and bs % 2 == 0
    mp = bp * sp
    nb_p = sp // MOBA_BLOCK
    assert nb_p <= 16
    hk = M_HEADS * M_QK_DIM
    hv = M_HEADS * M_V_DIM
    row = lambda a: a.reshape(1, -1)

    ii = np.arange(MOBA_BLOCK)
    bk_prompt = np.stack([_t5_bucket_np((nb_p - 1 - t) * MOBA_BLOCK + ii[:, None] - ii[None, :])
                          for t in range(nb_p)])
    past = n_pages * PAGE_SIZE
    kpos = np.arange(n_pages * PAGE_SIZE).reshape(n_pages, PAGE_SIZE)
    bk_step = np.concatenate([_t5_bucket_np(past - kpos), np.zeros((8, PAGE_SIZE), np.int32)])[None]
    bias_p = _bias_tables(rel_bias, bk_prompt, "bias_prompt")
    bias_s = _bias_tables(rel_bias, bk_step, "bias_step")

    xp = x_prompt.reshape(mp, d)
    xs = x_sample.reshape(bs, d)

    wqkv = w_qkv[0].astype(BF16)
    wo0 = w_attn_out[0].astype(BF16)
    q_p, kt_p, vt_p = _qkv_seq(xp, row(attn_norm[0]), wqkv[:, :d], wqkv[:, d:2 * d].T, wqkv[:, 2 * d:].T,
                               bp, sp, tm=512)
    attn_p = _moba_prompt(q_p.reshape(bp, sp, d), kt_p, vt_p, bias_p)

    ck_t = jnp.transpose(cache_k[0], (0, 2, 3, 1))
    cv_t = jnp.transpose(cache_v[0], (0, 2, 3, 1))
    half = bs // 2

    ffn_w = lambda i: (row(ffn_norm[i]), w_ffn_up[i].astype(BF16), ffn_conv_w[i], row(ffn_conv_b[i]),
                       w_ffn_down[i].astype(BF16))
    g0, wup0, cw0, cb0, wdn0 = ffn_w(0)
    gf = row(final_norm)
    hp1, conv_p0, ksum_a = _ffn(xp, attn_p.reshape(mp, d), wo0, g0, wup0, cw0, cb0, wdn0, gf, tm=512, seq_len=sp,
                                pages=(page_table[:half], ck_t, nblk))

    w_in = w_mlstm_in[0]
    w_in_bf = w_in[:, :2 * hk + 2 * hv].astype(BF16)
    wg_bf = jnp.pad(w_in[:, 2 * hk + 2 * hv:], ((0, 0), (0, LANES - 2 * M_HEADS))).astype(BF16)
    bg_col = b_mlstm_gate[0].reshape(2 * M_HEADS, 1)
    wo1 = w_mlstm_out[0].astype(BF16)
    gain = row(mlstm_head_gain[0])

    qk_p, vv_p, oo_p, gt_p = _mlstm_in(hp1, row(mlstm_norm[0]), w_in_bf, wg_bf, bg_col, tm=512, act_dtype=BF16)
    hc_p, c_p, n_p, m_p = _mlstm_seq(qk_p, vv_p, oo_p, gt_p, gain, bp, sp)

    g1, wup1, cw1, cb1, wdn1 = ffn_w(1)
    y_p, conv_p1, ksum_b = _ffn(hp1, hc_p.reshape(mp, d), wo1, g1, wup1, cw1, cb1, wdn1, gf, tm=512, seq_len=sp,
                                final=True, pages=(page_table[half:], ck_t, nblk))

    k_s, v_s, qt_s, kt_s, vt_s = _qkv_step(xs, row(attn_norm[0]), wqkv)
    sel = jnp.concatenate([_select(qt_s, ksum_a, nblk, 0), _select(qt_s, ksum_b, nblk, half)], axis=0)
    sel = sel[:, :, :MOBA_TOPK].reshape(bs, N_HEADS * MOBA_TOPK)
    attn_s_t = _decode_attn(sel, page_table, ck_t, cv_t, qt_s, kt_s, vt_s, bias_s)
    attn_s = attn_s_t.T[:bs].astype(BF16)
    hs1, conv_s0 = _ffn(xs, attn_s, wo0, g0, wup0, cw0, cb0, wdn0, gf, tm=bs, seq_len=1, state=state_conv[0])

    qk_s, vv_s, oo_s, gt_s = _mlstm_in(hs1, row(mlstm_norm[0]), w_in_bf, wg_bf, bg_col, tm=bs, act_dtype=F32)
    hc_s, c_s, n_s, m_s = _mlstm_step(qk_s, vv_s, oo_s, gt_s.T, gain, state_C[0], state_n[0], state_m[0])
    y_s, conv_s1 = _ffn(hs1, hc_s.reshape(bs, d), wo1, g1, wup1, cw1, cb1, wdn1, gf, tm=bs, seq_len=1,
                        state=state_conv[1], final=True)

    kv5 = lambda a, n: a.reshape(1, n, -1, N_HEADS, HEAD_DIM)
    kv5t = lambda a: jnp.transpose(a.reshape(1, bp, N_HEADS, HEAD_DIM, sp), (0, 1, 4, 2, 3))
    return (y_p.reshape(bp, sp, d), y_s.reshape(bs, ts, d),
            kv5t(kt_p), kv5t(vt_p), kv5(k_s, bs), kv5(v_s, bs),
            c_p[None], n_p.reshape(1, bp, M_HEADS, M_QK_DIM), m_p[:, :, 0, 0][None],
            c_s[None], n_s.reshape(1, bs, M_HEADS, M_QK_DIM), m_s.reshape(1, bs, M_HEADS),
            jnp.stack([conv_p0, conv_p1]), jnp.stack([conv_s0, conv_s1]))
```

```python
import functools
import math

import numpy as np
import jax
import jax.numpy as jnp
from jax import lax
from jax.experimental import pallas as pl
from jax.experimental.pallas import tpu as pltpu

F32 = jnp.float32
BF16 = jnp.bfloat16

D_MODEL = 1024
N_HEADS = 16
HEAD_DIM = 64
MOBA_BLOCK = 256
MOBA_TOPK = 3
PAGE_SIZE = 128
REL_BUCKETS = 32
REL_MAX_DIST = 4096
M_HEADS = 4
M_QK_DIM = 128
M_V_DIM = 256
D_FF = 2816
EPS = 1e-6
NEG_INF = -1e30
LANES = 128
MLSTM_L = 256
VMEM_LIMIT = 56 * 1024 * 1024
FFN_VMEM_LIMIT = 62 * 1024 * 1024


def _cparams(sem, vmem=VMEM_LIMIT):
    return pltpu.CompilerParams(dimension_semantics=sem, vmem_limit_bytes=vmem)


def _dot(a, b):
    return jnp.dot(a, b, preferred_element_type=F32)


def _dot_nt(a, b):
    return lax.dot_general(a, b, (((1,), (1,)), ((), ())), preferred_element_type=F32)


def _dot_tn(a, b):
    return lax.dot_general(a, b, (((0,), (0,)), ((), ())), preferred_element_type=F32)


def _rmsnorm(x, g):
    ms = jnp.mean(x * x, axis=-1, keepdims=True)
    return x * lax.rsqrt(ms + EPS) * g


def _t5_bucket_np(dist):
    n = np.maximum(dist, 0).astype(np.int64)
    max_exact = REL_BUCKETS // 2
    nf = np.maximum(n, 1).astype(np.float64)
    large = max_exact + np.floor(
        np.log(nf / max_exact) / math.log(REL_MAX_DIST / max_exact) * (REL_BUCKETS - max_exact) + 1e-9
    ).astype(np.int64)
    large = np.minimum(large, REL_BUCKETS - 1)
    return np.where(n < max_exact, n, large).astype(np.int32)


def _qkv_seq_kernel(x_ref, g_ref, wq_ref, wkt_ref, wvt_ref, q_ref, kt_ref, vt_ref):
    xn = _rmsnorm(x_ref[...], g_ref[...]).astype(BF16)
    d = D_MODEL
    chunk = 512
    for c in range(0, d, chunk):
        q_ref[:, c:c + chunk] = (_dot(xn, wq_ref[:, c:c + chunk]) * (HEAD_DIM ** -0.5)).astype(q_ref.dtype)
    for c in range(0, d, chunk):
        kt_ref[0, c:c + chunk, :] = _dot_nt(wkt_ref[c:c + chunk, :], xn)
    for c in range(0, d, chunk):
        vt_ref[0, c:c + chunk, :] = _dot_nt(wvt_ref[c:c + chunk, :], xn)


def _qkv_seq(x, g, wq_bf, wkt_bf, wvt_bf, nseq, seq_len, tm):
    m, d = x.shape
    tps = seq_len // tm
    const = lambda i: (0, 0)
    return pl.pallas_call(
        _qkv_seq_kernel,
        grid=(m // tm,),
        in_specs=[
            pl.BlockSpec((tm, d), lambda i: (i, 0)),
            pl.BlockSpec((1, d), const),
            pl.BlockSpec((d, d), const),
            pl.BlockSpec((d, d), const),
            pl.BlockSpec((d, d), const),
        ],
        out_specs=[
            pl.BlockSpec((tm, d), lambda i: (i, 0)),
            pl.BlockSpec((1, d, tm), lambda i: (i // tps, 0, i % tps)),
            pl.BlockSpec((1, d, tm), lambda i: (i // tps, 0, i % tps)),
        ],
        out_shape=[
            jax.ShapeDtypeStruct((m, d), BF16),
            jax.ShapeDtypeStruct((nseq, d, seq_len), F32),
            jax.ShapeDtypeStruct((nseq, d, seq_len), F32),
        ],
        compiler_params=_cparams(("arbitrary",)),
        name="qkv_seq",
    )(x, g, wq_bf, wkt_bf, wvt_bf)


def _mlstm_in_kernel(x_ref, g_ref, w_ref, wg_ref, bg_ref, qk_ref, v_ref, o_ref, gt_ref, *, tm, pad_rows):
    xn = _rmsnorm(x_ref[...], g_ref[...]).astype(BF16)
    hk = M_HEADS * M_QK_DIM
    hv = M_HEADS * M_V_DIM
    qk_ref[:, 0:hk] = (_dot(xn, w_ref[:, 0:hk]) * (M_QK_DIM ** -0.5)).astype(qk_ref.dtype)
    qk_ref[:, hk:2 * hk] = _dot(xn, w_ref[:, hk:2 * hk]).astype(qk_ref.dtype)
    for c in range(0, hv, 512):
        v_ref[:, c:c + 512] = _dot(xn, w_ref[:, 2 * hk + c:2 * hk + c + 512]).astype(v_ref.dtype)
        o_ref[:, c:c + 512] = _dot(xn, w_ref[:, 2 * hk + hv + c:2 * hk + hv + c + 512])
    gates = _dot(xn, wg_ref[...])
    if pad_rows:
        gates = jnp.concatenate([gates, jnp.zeros((pad_rows, LANES), F32)], axis=0)
    gt = gates.T
    gt_ref[...] = gt[0:8, 0:tm] + bg_ref[...]


def _mlstm_in(x, g, w_bf, wg_bf, bg_col, tm, act_dtype):
    m, d = x.shape
    pad_rows = (-tm) % LANES
    kern = functools.partial(_mlstm_in_kernel, tm=tm, pad_rows=pad_rows)
    return pl.pallas_call(
        kern,
        grid=(m // tm,),
        in_specs=[
            pl.BlockSpec((tm, d), lambda i: (i, 0)),
            pl.BlockSpec((1, d), lambda i: (0, 0)),
            pl.BlockSpec(w_bf.shape, lambda i: (0, 0)),
            pl.BlockSpec(wg_bf.shape, lambda i: (0, 0)),
            pl.BlockSpec((8, 1), lambda i: (0, 0)),
        ],
        out_specs=[
            pl.BlockSpec((tm, 1024), lambda i: (i, 0)),
            pl.BlockSpec((tm, 1024), lambda i: (i, 0)),
            pl.BlockSpec((tm, 1024), lambda i: (i, 0)),
            pl.BlockSpec((8, tm), lambda i: (0, i)),
        ],
        out_shape=[
            jax.ShapeDtypeStruct((m, 1024), act_dtype),
            jax.ShapeDtypeStruct((m, 1024), act_dtype),
            jax.ShapeDtypeStruct((m, 1024), F32),
            jax.ShapeDtypeStruct((8, m), F32),
        ],
        compiler_params=_cparams(("arbitrary",)),
        name="mlstm_in",
    )(x, g, w_bf, wg_bf, bg_col)


def _bias_table_kernel(rb_ref, bk_ref, o_ref, *, bucket_sets):
    h = pl.program_id(0)
    c = bk_ref.shape[2]
    for t, buckets in enumerate(bucket_sets):
        bk = bk_ref[t]
        acc = jnp.zeros(bk.shape, F32)
        for bucket in buckets:
            acc = jnp.where(bk == bucket, rb_ref[bucket, h], acc)
        o_ref[0, :, t * c:(t + 1) * c] = acc


def _bias_tables(rel_bias, bucket_np, name):
    nt, r, c = bucket_np.shape
    bucket_sets = tuple(tuple(int(v) for v in np.unique(bucket_np[t])) for t in range(nt))
    kern = functools.partial(_bias_table_kernel, bucket_sets=bucket_sets)
    return pl.pallas_call(
        kern,
        grid=(N_HEADS,),
        in_specs=[
            pl.BlockSpec(memory_space=pltpu.SMEM),
            pl.BlockSpec((nt, r, c), lambda h: (0, 0, 0)),
        ],
        out_specs=pl.BlockSpec((1, r, nt * c), lambda h: (h, 0, 0)),
        out_shape=jax.ShapeDtypeStruct((N_HEADS, r, nt * c), F32),
        compiler_params=_cparams(("arbitrary",)),
        name=name,
    )(rel_bias, jnp.asarray(bucket_np))


def _moba_prompt_kernel(q_ref, kt_ref, vt_ref, bias_ref, o_ref, kaug, vbf, s_scr, p_scr, *, seq, nb):
    blk = MOBA_BLOCK
    ktf = kt_ref[0]
    kt_hi = ktf.astype(BF16)
    kt_lo = (ktf - kt_hi.astype(F32)).astype(BF16)
    col_blk = lax.broadcasted_iota(jnp.int32, (LANES, seq), 1) // blk
    row_s = lax.broadcasted_iota(jnp.int32, (LANES, seq), 0)
    kaug[0:LANES, :] = kt_hi
    kaug[LANES:2 * LANES, :] = jnp.where(col_blk == row_s, 1.0, 0.0).astype(BF16)
    vbf[...] = vt_ref[0].astype(BF16)
    mean_w = jnp.where(lax.broadcasted_iota(jnp.int32, (16, seq), 1) // blk
                       == lax.broadcasted_iota(jnp.int32, (16, seq), 0), 1.0 / blk, 0.0).astype(BF16)
    km = _dot_nt(mean_w, kt_hi) + _dot_nt(mean_w, kt_lo)
    km_hi = km.astype(BF16)
    km_lo = (km - km_hi.astype(F32)).astype(BF16)

    lane_q = lax.broadcasted_iota(jnp.int32, (blk, LANES), 1)
    head_lanes = [(lane_q // HEAD_DIM) == h for h in range(2)]
    cand = lax.broadcasted_iota(jnp.int32, (16, blk), 0)
    eye = (lax.broadcasted_iota(jnp.int32, (blk, blk), 0)
           == lax.broadcasted_iota(jnp.int32, (blk, blk), 1)).astype(BF16)
    causal = (lax.broadcasted_iota(jnp.int32, (blk, blk), 1)
              <= lax.broadcasted_iota(jnp.int32, (blk, blk), 0))

    items = [(ob, h) for ob in range(nb) for h in range(2)]
    n_slots = s_scr.shape[0]
    state = {}

    def stage_logits(t):
        ob, h = items[t]
        slot = t % n_slots
        q0 = ob * blk
        qb = q_ref[0, q0:q0 + blk, :]
        qm = jnp.where(head_lanes[h], qb, jnp.zeros_like(qb))
        if ob <= MOBA_TOPK:
            lhs, k_rows = qm, LANES
        else:
            sc = _dot_nt(km_hi, qm) + _dot_nt(km_lo, qm)
            cnt = jnp.zeros((16, blk), F32)
            for m in range(ob):
                row = sc[m:m + 1, :]
                cnt = cnt + jnp.where(row > sc, 1.0, jnp.where((row == sc) & (cand > m), 1.0, 0.0))
            pen_t = jnp.where((cnt < MOBA_TOPK) | (cand >= ob), 0.0, NEG_INF).astype(BF16)
            pen_t = jnp.concatenate([pen_t, jnp.zeros((LANES - 16, blk), BF16)], axis=0)
            pen = _dot_nt(eye, pen_t).astype(BF16)
            lhs, k_rows = jnp.concatenate([qm, pen], axis=1), 2 * LANES
        m_part = None
        for n in range(ob + 1):
            cols = slice(n * blk, (n + 1) * blk)
            s_n = _dot(lhs, kaug[0:k_rows, cols]) + bias_ref[h, :, (nb - 1 - ob + n) * blk:(nb - ob + n) * blk]
            if n == ob:
                s_n = jnp.where(causal, s_n, NEG_INF)
            s_scr[slot, :, cols] = s_n
            t_max = jnp.maximum(s_n[:, 0:LANES], s_n[:, LANES:])
            m_part = t_max if m_part is None else jnp.maximum(m_part, t_max)
            if n < ob:
                yield
        state[t] = {"m": jnp.max(m_part, axis=-1, keepdims=True)}
        yield

    def stage_softmax(t):
        ob, h = items[t]
        slot = t % n_slots
        m_i = state[t]["m"]
        l_part = None
        for n in range(ob + 1):
            cols = slice(n * blk, (n + 1) * blk)
            p_n = jnp.exp(s_scr[slot, :, cols] - m_i)
            p_scr[slot, :, cols] = p_n.astype(p_scr.dtype)
            t_sum = p_n[:, 0:LANES] + p_n[:, LANES:]
            l_part = t_sum if l_part is None else l_part + t_sum
            if n < ob:
                yield
        state[t]["l"] = jnp.sum(l_part, axis=-1, keepdims=True)
        yield

    def stage_pv(t):
        ob, h = items[t]
        slot = t % n_slots
        acc = None
        for n in range(ob + 1):
            cols = slice(n * blk, (n + 1) * blk)
            part = _dot_nt(vbf[:, cols], p_scr[slot, :, cols])
            acc = part if acc is None else acc + part
            if n < ob:
                yield
        out = acc.T / state[t]["l"]
        if h == 0:
            state[t]["out"] = out
        else:
            q0 = ob * blk
            o_ref[0, q0:q0 + blk, :] = jnp.where(head_lanes[0], state[t - 1]["out"], out).astype(o_ref.dtype)
            del state[t - 1], state[t]
        yield

    for t in range(len(items) + 2):
        gens = []
        if t < len(items):
            gens.append(stage_logits(t))
        if 0 <= t - 1 < len(items):
            gens.append(stage_softmax(t - 1))
        if 0 <= t - 2 < len(items):
            gens.append(stage_pv(t - 2))
        while gens:
            for gen in list(gens):
                if next(gen, "done") == "done":
                    gens.remove(gen)


def _moba_prompt(q, kt, vt, bias_strip):
    b, s, d = q.shape
    nb = s // MOBA_BLOCK
    hp = d // LANES
    kern = functools.partial(_moba_prompt_kernel, seq=s, nb=nb)
    return pl.pallas_call(
        kern,
        grid=(hp, b),
        in_specs=[
            pl.BlockSpec((1, s, LANES), lambda p, i: (i, 0, p)),
            pl.BlockSpec((1, LANES, s), lambda p, i: (i, p, 0)),
            pl.BlockSpec((1, LANES, s), lambda p, i: (i, p, 0)),
            pl.BlockSpec((2, MOBA_BLOCK, s), lambda p, i: (p, 0, 0)),
        ],
        out_specs=pl.BlockSpec((1, s, LANES), lambda p, i: (i, 0, p)),
        out_shape=jax.ShapeDtypeStruct((b, s, d), BF16),
        scratch_shapes=[pltpu.VMEM((2 * LANES, s), BF16), pltpu.VMEM((LANES, s), BF16),
                        pltpu.VMEM((4, MOBA_BLOCK, s), F32), pltpu.VMEM((4, MOBA_BLOCK, s), BF16)],
        compiler_params=_cparams(("arbitrary", "arbitrary")),
        name="moba_prompt",
    )(q, kt, vt, bias_strip)


class _KmPlan:
    def __init__(self, n_steps, n_seq, nblk):
        total = n_seq * nblk
        assert total % n_steps == 0
        self.n_seq, self.nblk, self.total = n_seq, nblk, total
        self.per_step = total // n_steps
        assert nblk % self.per_step == 0
        self.steps_per_seq = nblk // self.per_step
        self.group = next(g for g in (4, 2, 1) if self.per_step % g == 0)
        self.points = self.per_step // self.group
        self.ahead = 3 * self.group
        self.nbuf = self.ahead + self.group


def _ffn_kernel(*refs, tm, tiles_per_seq, seq_mode, final, km=None):
    if km is not None:
        pt_ref, refs = refs[0], refs[1:]
        (x_ref, a_ref, wo_ref, g_ref, wup_ref, cw_ref, cb_ref, wdn_ref, gf_ref, ck_ref,
         y_ref, cs_ref, km_ref, hbuf, carry, kbuf, ksem) = refs
    elif seq_mode:
        (x_ref, a_ref, wo_ref, g_ref, wup_ref, cw_ref, cb_ref, wdn_ref, gf_ref,
         y_ref, cs_ref, hbuf, carry) = refs
    else:
        (x_ref, a_ref, wo_ref, g_ref, wup_ref, cw_ref, cb_ref, wdn_ref, gf_ref, st0_ref, st1_ref,
         y_ref, cs0_ref, cs1_ref, hbuf) = refs
    i = pl.program_id(0)
    cf = 256
    n_chunks = D_FF // cf

    if km is not None:
        ppb = MOBA_BLOCK // PAGE_SIZE
        lane_km = lax.broadcasted_iota(jnp.int32, (N_HEADS * HEAD_DIM, LANES), 1)

        def km_copies(g):
            slot = g % km.nbuf
            sq = g // km.nblk
            n = g % km.nblk
            return [pltpu.make_async_copy(ck_ref.at[pt_ref[sq, ppb * n + j]], kbuf.at[slot, j], ksem.at[slot, j])
                    for j in range(ppb)]

        @pl.when(i == 0)
        def _():
            for g in range(km.ahead):
                for cp in km_copies(g):
                    cp.start()

        @pl.when(i % km.steps_per_seq == 0)
        def _():
            km_ref[...] = jnp.zeros_like(km_ref)

        def km_wait(p):
            g0 = i * km.per_step + p * km.group
            for u in range(km.group):
                for cp in km_copies(g0 + u):
                    cp.wait()
            for u in range(km.group):
                @pl.when(g0 + km.ahead + u < km.total)
                def _():
                    for cp in km_copies(g0 + km.ahead + u):
                        cp.start()

        def km_reduce(p):
            g0 = i * km.per_step + p * km.group
            acc = km_ref[0]
            for u in range(km.group):
                slot = (g0 + u) % km.nbuf
                x = kbuf[slot, 0]
                for j in range(1, ppb):
                    x = x + kbuf[slot, j]
                ssum = jnp.sum(x.reshape(N_HEADS * HEAD_DIM, PAGE_SIZE), axis=-1, keepdims=True)
                acc = jnp.where(lane_km == (g0 + u) % km.nblk, ssum, acc)
            km_ref[0] = acc

        km_at = {}
        for p in range(km.points):
            km_at.setdefault((p * n_chunks) // km.points, []).append(p)

    x1 = x_ref[...] + _dot(a_ref[...], wo_ref[...])
    xn = _rmsnorm(x1, g_ref[...]).astype(BF16)
    if seq_mode:
        @pl.when(i % tiles_per_seq == 0)
        def _():
            carry[...] = jnp.zeros_like(carry)
        row = lax.broadcasted_iota(jnp.int32, (tm, 256), 0)
    for ci, c in enumerate(range(0, D_FF, cf)):
        if km is not None:
            for p in km_at.get(ci, ()):
                if p > 0:
                    km_reduce(p - 1)
                km_wait(p)
        ug = _dot(xn, wup_ref[:, c:c + cf])
        uv = _dot(xn, wup_ref[:, D_FF + c:D_FF + c + cf])
        if seq_mode:
            c0 = carry[0:1, c:c + cf]
            c1 = carry[1:2, c:c + cf]
            p1 = jnp.where(row == 0, c1, pltpu.roll(ug, 1, axis=0))
            p2 = jnp.where(row == 0, c0, jnp.where(row == 1, c1, pltpu.roll(ug, 2, axis=0)))
            last2 = ug[tm - 2:tm, :]
            carry[0:2, c:c + cf] = last2
            cs_ref[0, :, c:c + cf] = last2
        else:
            p2 = st0_ref[:, c:c + cf]
            p1 = st1_ref[:, c:c + cf]
            cs0_ref[:, c:c + cf] = p1
            cs1_ref[:, c:c + cf] = ug
        conv = (cb_ref[:, c:c + cf] + cw_ref[0:1, c:c + cf] * p2 + cw_ref[1:2, c:c + cf] * p1
                + cw_ref[2:3, c:c + cf] * ug)
        hbuf[:, c:c + cf] = (conv * (1.0 / (1.0 + jnp.exp(-conv))) * uv).astype(hbuf.dtype)
    if km is not None:
        km_reduce(km.points - 1)
    y = x1 + _dot(hbuf[...], wdn_ref[...])
    if final:
        y = _rmsnorm(y, gf_ref[...])
    y_ref[...] = y


def _ffn(x, a, wo_bf, g, wup_bf, cw, cb, wdn_bf, gf, *, tm, seq_len, state=None, final=False, pages=None):
    m, d = x.shape
    seq_mode = state is None
    tiles_per_seq = (seq_len // tm) if seq_mode else 1
    n_steps = m // tm
    km = None
    if pages is not None:
        pt_rows, ck_t, nblk = pages
        km = _KmPlan(n_steps, pt_rows.shape[0], nblk)
    kern = functools.partial(_ffn_kernel, tm=tm, tiles_per_seq=tiles_per_seq, seq_mode=seq_mode, final=final,
                             km=km)
    const = lambda i, *_: (0, 0)
    rows = lambda i, *_: (i, 0)
    resident = functools.partial(pl.BlockSpec, pipeline_mode=pl.Buffered(1))
    in_specs = [
        pl.BlockSpec((tm, d), rows),
        pl.BlockSpec((tm, d), rows),
        resident(wo_bf.shape, const),
        pl.BlockSpec((1, d), const),
        resident(wup_bf.shape, const),
        pl.BlockSpec((3, D_FF), const),
        pl.BlockSpec((1, D_FF), const),
        resident(wdn_bf.shape, const),
        pl.BlockSpec((1, d), const),
    ]
    args = [x, a, wo_bf, g, wup_bf, cw, cb, wdn_bf, gf]
    scratch = [pltpu.VMEM((tm, D_FF), BF16)]
    if seq_mode:
        nseq = m // seq_len
        cs_shapes = [jax.ShapeDtypeStruct((nseq, 2, D_FF), F32)]
        cs_specs = [pl.BlockSpec((1, 2, D_FF), lambda i, *_: (i // tiles_per_seq, 0, 0))]
        scratch.append(pltpu.VMEM((8, D_FF), F32))
    else:
        in_specs += [pl.BlockSpec((tm, D_FF), rows)] * 2
        args += [state[:, 0], state[:, 1]]
        cs_shapes = [jax.ShapeDtypeStruct((m, D_FF), F32)] * 2
        cs_specs = [pl.BlockSpec((tm, D_FF), rows)] * 2
    n_prefetch = 0
    if km is not None:
        assert seq_mode
        ppb = MOBA_BLOCK // PAGE_SIZE
        n_prefetch = 1
        in_specs.append(pl.BlockSpec(memory_space=pl.ANY))
        args = [pt_rows] + args + [ck_t]
        cs_shapes.append(jax.ShapeDtypeStruct((km.n_seq, N_HEADS * HEAD_DIM, LANES), F32))
        cs_specs.append(pl.BlockSpec((1, N_HEADS * HEAD_DIM, LANES),
                                     lambda i, *_: (i // km.steps_per_seq, 0, 0)))
        scratch += [pltpu.VMEM((km.nbuf, ppb, N_HEADS, HEAD_DIM, PAGE_SIZE), F32),
                    pltpu.SemaphoreType.DMA((km.nbuf, ppb))]
    outs = pl.pallas_call(
        kern,
        grid_spec=pltpu.PrefetchScalarGridSpec(
            num_scalar_prefetch=n_prefetch,
            grid=(n_steps,),
            in_specs=in_specs,
            out_specs=[pl.BlockSpec((tm, d), rows)] + cs_specs,
            scratch_shapes=scratch,
        ),
        out_shape=[jax.ShapeDtypeStruct((m, d), F32)] + cs_shapes,
        compiler_params=_cparams(("arbitrary",), vmem=FFN_VMEM_LIMIT if km is not None else VMEM_LIMIT),
        name="ffn_seq" if seq_mode else "ffn_step",
    )(*args)
    if not seq_mode:
        return outs[0], jnp.stack([outs[1], outs[2]], axis=1)
    return tuple(outs)


def _log_sigmoid(x):
    return jnp.minimum(x, 0.0) - jnp.log1p(jnp.exp(-jnp.abs(x)))


def _mlstm_seq_kernel(q_ref, k_ref, v_ref, o_ref, gt_ref, gain_ref,
                      h_ref, c_out, n_out, m_out, gate_s, *, seq):
    L = MLSTM_L
    hd = pl.program_id(1)
    tri = (lax.broadcasted_iota(jnp.int32, (L, L), 1) <= lax.broadcasted_iota(jnp.int32, (L, L), 0))
    diag = (lax.broadcasted_iota(jnp.int32, (L, L), 1) == lax.broadcasted_iota(jnp.int32, (L, L), 0))

    ig_all = gt_ref[pl.ds(hd, 1), :]
    lf_all = _log_sigmoid(gt_ref[pl.ds(M_HEADS + hd, 1), :])
    pos = lax.broadcasted_iota(jnp.int32, (1, seq), 1) % L
    bc_all = lf_all
    sh = 1
    while sh < L:
        bc_all = bc_all + jnp.where(pos >= sh, pltpu.roll(bc_all, sh, axis=1), 0.0)
        sh *= 2
    gate_s[0:1, :] = ig_all
    gate_s[1:2, :] = lf_all
    gate_s[2:3, :] = ig_all - bc_all

    c_st = jnp.zeros((M_QK_DIM, M_V_DIM), F32)
    n_st = jnp.zeros((1, M_QK_DIM), F32)
    m_prev = jnp.zeros((1, 1), F32)
    for ci in range(seq // L):
        t0 = ci * L
        ig = gate_s[0:1, t0:t0 + L]
        lf = gate_s[1:2, t0:t0 + L]
        u = gate_s[2:3, t0:t0 + L]
        cmax_c = jnp.max(jnp.where(tri, u, -jnp.inf), axis=-1, keepdims=True)
        bc_c = jnp.sum(jnp.where(tri, lf, 0.0), axis=-1, keepdims=True)
        ig_c = jnp.sum(jnp.where(diag, ig, 0.0), axis=-1, keepdims=True)
        big_m = jnp.maximum(m_prev, cmax_c)
        dmat = jnp.exp(jnp.where(tri, u - big_m, -jnp.inf))
        inter = jnp.exp(m_prev - big_m)
        mt_c = bc_c + big_m

        q = q_ref[0, t0:t0 + L, :]
        k = k_ref[0, t0:t0 + L, :]
        v = v_ref[0, t0:t0 + L, :]
        sm = _dot_nt(q, k) * dmat
        num = _dot(sm.astype(BF16), v) + inter * _dot(q, c_st.astype(BF16))
        qn = jnp.sum(q.astype(F32) * n_st, axis=-1, keepdims=True)
        den = jnp.sum(sm, axis=-1, keepdims=True) + inter * qn
        hout = num / jnp.maximum(jnp.abs(den), jnp.exp(-mt_c))

        bc_last = jnp.sum(lf, axis=-1, keepdims=True)
        m_new = bc_last + jnp.maximum(m_prev, jnp.max(u, axis=-1, keepdims=True))
        decay = jnp.exp(bc_last + m_prev - m_new)
        w_r = jnp.exp(u + bc_last - m_new)
        w_c = jnp.exp(ig_c - bc_c + bc_last - m_new)
        vw = (v.astype(F32) * w_c).astype(BF16)
        c_st = decay * c_st + _dot_tn(k, vw)
        w8 = jnp.broadcast_to(w_r, (8, L)).astype(BF16)
        n_st = decay * n_st + _dot(w8, k)[0:1, :]
        m_prev = m_new

        hn = hout * lax.rsqrt(jnp.mean(hout * hout, axis=-1, keepdims=True) + EPS)
        o = o_ref[0, t0:t0 + L, :]
        hn = hn * gain_ref[...] * (1.0 / (1.0 + jnp.exp(-o)))
        h_ref[0, t0:t0 + L, :] = hn.astype(h_ref.dtype)

    c_out[0, 0] = c_st
    n_out[0, 0] = n_st
    m_out[0, 0] = jnp.broadcast_to(m_prev, (1, LANES))


def _mlstm_seq(qk, v, o, gt, gain, b, s):
    kern = functools.partial(_mlstm_seq_kernel, seq=s)
    qk3 = qk.reshape(b, s, 2 * M_HEADS * M_QK_DIM)
    v3 = v.reshape(b, s, M_HEADS * M_V_DIM)
    o3 = o.reshape(b, s, M_HEADS * M_V_DIM)
    return pl.pallas_call(
        kern,
        grid=(b, M_HEADS),
        in_specs=[
            pl.BlockSpec((1, s, M_QK_DIM), lambda i, h: (i, 0, h)),
            pl.BlockSpec((1, s, M_QK_DIM), lambda i, h: (i, 0, M_HEADS + h)),
            pl.BlockSpec((1, s, M_V_DIM), lambda i, h: (i, 0, h)),
            pl.BlockSpec((1, s, M_V_DIM), lambda i, h: (i, 0, h)),
            pl.BlockSpec((8, s), lambda i, h: (0, i)),
            pl.BlockSpec((1, M_V_DIM), lambda i, h: (0, h)),
        ],
        out_specs=[
            pl.BlockSpec((1, s, M_V_DIM), lambda i, h: (i, 0, h)),
            pl.BlockSpec((1, 1, M_QK_DIM, M_V_DIM), lambda i, h: (i, h, 0, 0)),
            pl.BlockSpec((1, 1, 1, M_QK_DIM), lambda i, h: (i, h, 0, 0)),
            pl.BlockSpec((1, 1, 1, LANES), lambda i, h: (i, h, 0, 0)),
        ],
        out_shape=[
            jax.ShapeDtypeStruct((b, s, M_HEADS * M_V_DIM), BF16),
            jax.ShapeDtypeStruct((b, M_HEADS, M_QK_DIM, M_V_DIM), F32),
            jax.ShapeDtypeStruct((b, M_HEADS, 1, M_QK_DIM), F32),
            jax.ShapeDtypeStruct((b, M_HEADS, 1, LANES), F32),
        ],
        scratch_shapes=[pltpu.VMEM((8, s), F32)],
        compiler_params=_cparams(("arbitrary", "arbitrary")),
        name="mlstm_seq",
    )(qk3, qk3, v3, o3, gt, gain)


def _mlstm_step_kernel(qk_ref, v_ref, o_ref, g_ref, gain_ref, c_ref, n_ref, m_ref,
                       h_ref, c_out, n_out, m_out):
    eye = (lax.broadcasted_iota(jnp.int32, (M_QK_DIM, M_QK_DIM), 0)
           == lax.broadcasted_iota(jnp.int32, (M_QK_DIM, M_QK_DIM), 1))
    hk = M_HEADS * M_QK_DIM
    for h in range(M_HEADS):
        q = qk_ref[0, :, h * M_QK_DIM:(h + 1) * M_QK_DIM]
        k = qk_ref[0, :, hk + h * M_QK_DIM:hk + (h + 1) * M_QK_DIM]
        v = v_ref[0, :, h * M_V_DIM:(h + 1) * M_V_DIM]
        o = o_ref[0, :, h * M_V_DIM:(h + 1) * M_V_DIM]
        ig = g_ref[0, :, h:h + 1]
        lf = _log_sigmoid(g_ref[0, :, M_HEADS + h:M_HEADS + h + 1])
        m_prev = m_ref[0, :, h:h + 1]
        c = c_ref[0, h]
        n = n_ref[0, h]
        q_col = jnp.sum(jnp.where(eye, q, 0.0), axis=-1, keepdims=True)
        k_col = jnp.sum(jnp.where(eye, k, 0.0), axis=-1, keepdims=True)
        g = lf + m_prev
        mt = jnp.maximum(g, ig)
        dm = jnp.exp(ig - mt)
        inter = jnp.exp(g - mt)
        sm = jnp.sum(q * k, axis=-1, keepdims=True) * dm
        qc = jnp.sum(q_col * c, axis=0, keepdims=True)
        num = sm * v + inter * qc
        den = sm + inter * jnp.sum(q * n, axis=-1, keepdims=True)
        hout = num / jnp.maximum(jnp.abs(den), jnp.exp(-mt))
        c_out[0, h] = inter * c + dm * (k_col * v)
        n_out[0, h] = inter * n + dm * k
        m_out[0, :, h:h + 1] = mt
        hn = hout * lax.rsqrt(jnp.mean(hout * hout, axis=-1, keepdims=True) + EPS)
        hn = hn * gain_ref[:, h * M_V_DIM:(h + 1) * M_V_DIM] * (1.0 / (1.0 + jnp.exp(-o)))
        h_ref[0, :, h * M_V_DIM:(h + 1) * M_V_DIM] = hn.astype(h_ref.dtype)


def _mlstm_step(qk, v, o, g_rows, gain, c0, n0, m0):
    nb = qk.shape[0]
    hv = M_HEADS * M_V_DIM
    row3 = lambda a: a.reshape(nb, 1, a.shape[-1])
    spec3 = lambda w: pl.BlockSpec((1, 1, w), lambda i: (i, 0, 0))
    return pl.pallas_call(
        _mlstm_step_kernel,
        grid=(nb,),
        in_specs=[
            spec3(1024), spec3(hv), spec3(hv), spec3(8),
            pl.BlockSpec((1, hv), lambda i: (0, 0)),
            pl.BlockSpec((1, M_HEADS, M_QK_DIM, M_V_DIM), lambda i: (i, 0, 0, 0)),
            pl.BlockSpec((1, M_HEADS, 1, M_QK_DIM), lambda i: (i, 0, 0, 0)),
            spec3(M_HEADS),
        ],
        out_specs=[
            spec3(hv),
            pl.BlockSpec((1, M_HEADS, M_QK_DIM, M_V_DIM), lambda i: (i, 0, 0, 0)),
            pl.BlockSpec((1, M_HEADS, 1, M_QK_DIM), lambda i: (i, 0, 0, 0)),
            spec3(M_HEADS),
        ],
        out_shape=[
            jax.ShapeDtypeStruct((nb, 1, hv), BF16),
            jax.ShapeDtypeStruct((nb, M_HEADS, M_QK_DIM, M_V_DIM), F32),
            jax.ShapeDtypeStruct((nb, M_HEADS, 1, M_QK_DIM), F32),
            jax.ShapeDtypeStruct((nb, 1, M_HEADS), F32),
        ],
        compiler_params=_cparams(("arbitrary",)),
        name="mlstm_step",
    )(row3(qk), row3(v), row3(o), row3(g_rows), gain, c0,
      n0.reshape(nb, M_HEADS, 1, M_QK_DIM), row3(m0))


def _select_kernel(qt_ref, km_ref, sel_ref, *, nblk, seq_base):
    b = pl.program_id(0) + seq_base
    lane_q = lax.broadcasted_iota(jnp.int32, (N_HEADS * HEAD_DIM, LANES), 1)
    q_col = jnp.sum(jnp.where(lane_q == b, qt_ref[...], 0.0), axis=-1, keepdims=True)
    prod = (km_ref[0] * q_col).reshape(N_HEADS, HEAD_DIM, LANES)
    sc = jnp.sum(prod, axis=1) * (1.0 / MOBA_BLOCK)
    lane = lax.broadcasted_iota(jnp.int32, (N_HEADS, LANES), 1)
    lane_f = lane.astype(F32)
    sc = jnp.where(lane < nblk, sc, -jnp.inf)
    out = jnp.zeros((N_HEADS, LANES), jnp.int32)
    for r in range(MOBA_TOPK):
        mx = jnp.max(sc, axis=-1, keepdims=True)
        idx = jnp.min(jnp.where(sc == mx, lane_f, float(LANES)), axis=-1, keepdims=True)
        out = jnp.where(lane == r, idx.astype(jnp.int32), out)
        sc = jnp.where(lane_f == idx, -jnp.inf, sc)
    sel_ref[0] = out


def _select(qt, kmean_t, nblk, seq_base):
    n_seq = kmean_t.shape[0]
    kern = functools.partial(_select_kernel, nblk=nblk, seq_base=seq_base)
    return pl.pallas_call(
        kern,
        grid=(n_seq,),
        in_specs=[
            pl.BlockSpec(qt.shape, lambda b: (0, 0)),
            pl.BlockSpec((1, N_HEADS * HEAD_DIM, LANES), lambda b: (b, 0, 0)),
        ],
        out_specs=pl.BlockSpec((1, N_HEADS, LANES), lambda b: (b, 0, 0)),
        out_shape=jax.ShapeDtypeStruct((n_seq, N_HEADS, LANES), jnp.int32),
        compiler_params=_cparams(("arbitrary",)),
        name="moba_select",
    )(qt, kmean_t)


def _decode_attn_kernel(sel_ref, pt_ref, ck_ref, cv_ref, qt_ref, kt_ref, vt_ref, bias_ref, o_ref,
                        kbuf, vbuf, sem, *, n_pages, n_seq):
    b = pl.program_id(0)
    ppb = MOBA_BLOCK // PAGE_SIZE
    n_t = MOBA_TOPK * ppb
    self_row = n_pages
    slot = b % 2

    def copies(bb, sl, h, t):
        r, j = divmod(t, ppb)
        lp = ppb * sel_ref[bb, h * MOBA_TOPK + r] + j
        page = pt_ref[bb, lp]
        return (pltpu.make_async_copy(ck_ref.at[page, h], kbuf.at[sl, h, t], sem.at[sl, 0, h, t]),
                pltpu.make_async_copy(cv_ref.at[page, h], vbuf.at[sl, h, t], sem.at[sl, 1, h, t]))

    def start_all(bb, sl):
        def per_head(h, carry):
            for t in range(n_t):
                for cp in copies(bb, sl, h, t):
                    cp.start()
            return carry
        lax.fori_loop(0, N_HEADS, per_head, 0)

    @pl.when(b == 0)
    def _():
        o_ref[...] = jnp.zeros_like(o_ref)
        start_all(0, 0)

    @pl.when(b + 1 < n_seq)
    def _():
        start_all(b + 1, 1 - slot)

    def wait_head(h, carry):
        for t in range(n_t):
            for cp in copies(b, slot, h, t):
                cp.wait()
        return carry
    lax.fori_loop(0, N_HEADS, wait_head, 0)

    lane_b = lax.broadcasted_iota(jnp.int32, (N_HEADS * HEAD_DIM, LANES), 1) == b
    pick = lambda ref: jnp.sum(jnp.where(lane_b, ref[...], 0.0), axis=-1, keepdims=True)
    q_col = pick(qt_ref)
    k_col = pick(kt_ref)
    v_col = pick(vt_ref)
    lane_o = lax.broadcasted_iota(jnp.int32, (HEAD_DIM, LANES), 1) == b

    head_rows = lambda h: slice(h * HEAD_DIM, (h + 1) * HEAD_DIM)
    s_tiles = []
    for t in range(n_t):
        r, j = divmod(t, ppb)
        per_head = []
        for h in range(N_HEADS):
            lp = ppb * sel_ref[b, h * MOBA_TOPK + r] + j
            per_head.append(jnp.sum(kbuf[slot, h, t] * q_col[head_rows(h)], axis=0, keepdims=True)
                            + bias_ref[h, pl.ds(lp, 1), :])
        s_tiles.append(jnp.concatenate(per_head, axis=0))
    s_self = (jnp.sum((q_col * k_col).reshape(N_HEADS, HEAD_DIM, 1), axis=1)
              + bias_ref[:, self_row:self_row + 1, 0:1].reshape(N_HEADS, 1))
    m = s_self
    for s_t in s_tiles:
        m = jnp.maximum(m, jnp.max(s_t, axis=-1, keepdims=True))
    p_self = jnp.exp(s_self - m)
    p_tiles = [jnp.exp(s_t - m) for s_t in s_tiles]
    l = p_self
    for p_t in p_tiles:
        l = l + jnp.sum(p_t, axis=-1, keepdims=True)
    for h in range(N_HEADS):
        acc = vbuf[slot, h, 0] * p_tiles[0][h:h + 1, :]
        for t in range(1, n_t):
            acc = acc + vbuf[slot, h, t] * p_tiles[t][h:h + 1, :]
        o_col = ((jnp.sum(acc, axis=-1, keepdims=True) + p_self[h:h + 1, :] * v_col[head_rows(h)])
                 / l[h:h + 1, :])
        o_ref[head_rows(h), :] = jnp.where(lane_o, o_col, o_ref[head_rows(h), :])


def _decode_attn(sel, page_table, ck_t, cv_t, qt, kt, vt, bias_s):
    n_seq, n_pages = page_table.shape
    kern = functools.partial(_decode_attn_kernel, n_pages=n_pages, n_seq=n_seq)
    n_t = MOBA_TOPK * (MOBA_BLOCK // PAGE_SIZE)
    full = lambda a: pl.BlockSpec(a.shape, lambda b, s, p: (0,) * a.ndim)
    return pl.pallas_call(
        kern,
        grid_spec=pltpu.PrefetchScalarGridSpec(
            num_scalar_prefetch=2,
            grid=(n_seq,),
            in_specs=[pl.BlockSpec(memory_space=pl.ANY), pl.BlockSpec(memory_space=pl.ANY),
                      full(qt), full(kt), full(vt), full(bias_s)],
            out_specs=pl.BlockSpec((N_HEADS * HEAD_DIM, LANES), lambda b, s, p: (0, 0)),
            scratch_shapes=[pltpu.VMEM((2, N_HEADS, n_t, HEAD_DIM, PAGE_SIZE), F32),
                            pltpu.VMEM((2, N_HEADS, n_t, HEAD_DIM, PAGE_SIZE), F32),
                            pltpu.SemaphoreType.DMA((2, 2, N_HEADS, n_t))],
        ),
        out_shape=jax.ShapeDtypeStruct((N_HEADS * HEAD_DIM, LANES), F32),
        compiler_params=_cparams(("arbitrary",)),
        name="moba_decode",
    )(sel, page_table, ck_t, cv_t, qt, kt, vt, bias_s)


def _qkv_step_kernel(x_ref, g_ref, w_ref, k_ref, v_ref, qt_ref, kt_ref, vt_ref, *, nb):
    xn = _rmsnorm(x_ref[...], g_ref[...]).astype(BF16)
    d = D_MODEL
    pad = jnp.zeros((LANES - nb, d), F32)
    q = _dot(xn, w_ref[:, 0:d]) * (HEAD_DIM ** -0.5)
    k = _dot(xn, w_ref[:, d:2 * d])
    v = _dot(xn, w_ref[:, 2 * d:3 * d])
    k_ref[...] = k
    v_ref[...] = v
    qt_ref[...] = jnp.concatenate([q, pad], axis=0).T
    kt_ref[...] = jnp.concatenate([k, pad], axis=0).T
    vt_ref[...] = jnp.concatenate([v, pad], axis=0).T


def _qkv_step(x, g, w_bf):
    nb, d = x.shape
    kern = functools.partial(_qkv_step_kernel, nb=nb)
    full = lambda shape: pl.BlockSpec(shape, lambda i: (0,) * len(shape))
    return pl.pallas_call(
        kern,
        grid=(1,),
        in_specs=[full((nb, d)), full((1, d)), full(w_bf.shape)],
        out_specs=[full((nb, d)), full((nb, d)), full((d, LANES)), full((d, LANES)), full((d, LANES))],
        out_shape=[jax.ShapeDtypeStruct((nb, d), F32), jax.ShapeDtypeStruct((nb, d), F32),
                   jax.ShapeDtypeStruct((d, LANES), F32), jax.ShapeDtypeStruct((d, LANES), F32),
                   jax.ShapeDtypeStruct((d, LANES), F32)],
        compiler_params=_cparams(("arbitrary",)),
        name="qkv_step",
    )(x, g, w_bf)


def kernel(x_prompt, x_sample, cache_k, cache_v, state_C, state_n, state_m, state_conv, page_table, rel_bias, attn_norm, w_qkv, w_attn_out, mlstm_norm, w_mlstm_in, b_mlstm_gate, mlstm_head_gain, w_mlstm_out, ffn_norm, w_ffn_up, ffn_conv_w, ffn_conv_b, w_ffn_down, final_norm):
    bp, sp, d = x_prompt.shape
    bs, ts, _ = x_sample.shape
    assert ts == 1 and d == D_MODEL and sp % MOBA_BLOCK == 0 and sp % MLSTM_L == 0
    n_pages = page_table.shape[1]
    assert (n_pages * PAGE_SIZE) % MOBA_BLOCK == 0
    nblk = n_pages * PAGE_SIZE // MOBA_BLOCK
    assert MOBA_TOPK <= nblk <= LANES and bs % 2 == 0
    mp = bp * sp
    nb_p = sp // MOBA_BLOCK
    assert nb_p <= 16
    hk = M_HEADS * M_QK_DIM
    hv = M_HEADS * M_V_DIM
    row = lambda a: a.reshape(1, -1)

    ii = np.arange(MOBA_BLOCK)
    bk_prompt = np.stack([_t5_bucket_np((nb_p - 1 - t) * MOBA_BLOCK + ii[:, None] - ii[None, :])
                          for t in range(nb_p)])
    past = n_pages * PAGE_SIZE
    kpos = np.arange(n_pages * PAGE_SIZE).reshape(n_pages, PAGE_SIZE)
    bk_step = np.concatenate([_t5_bucket_np(past - kpos), np.zeros((8, PAGE_SIZE), np.int32)])[None]
    bias_p = _bias_tables(rel_bias, bk_prompt, "bias_prompt")
    bias_s = _bias_tables(rel_bias, bk_step, "bias_step")

    xp = x_prompt.reshape(mp, d)
    xs = x_sample.reshape(bs, d)

    wqkv = w_qkv[0].astype(BF16)
    wo0 = w_attn_out[0].astype(BF16)
    q_p, kt_p, vt_p = _qkv_seq(xp, row(attn_norm[0]), wqkv[:, :d], wqkv[:, d:2 * d].T, wqkv[:, 2 * d:].T,
                               bp, sp, tm=512)
    attn_p = _moba_prompt(q_p.reshape(bp, sp, d), kt_p, vt_p, bias_p)

    ck_t = jnp.transpose(cache_k[0], (0, 2, 3, 1))
    cv_t = jnp.transpose(cache_v[0], (0, 2, 3, 1))
    half = bs // 2

    ffn_w = lambda i: (row(ffn_norm[i]), w_ffn_up[i].astype(BF16), ffn_conv_w[i], row(ffn_conv_b[i]),
                       w_ffn_down[i].astype(BF16))
    g0, wup0, cw0, cb0, wdn0 = ffn_w(0)
    gf = row(final_norm)
    hp1, conv_p0, ksum_a = _ffn(xp, attn_p.reshape(mp, d), wo0, g0, wup0, cw0, cb0, wdn0, gf, tm=512, seq_len=sp,
                                pages=(page_table[:half], ck_t, nblk))

    w_in = w_mlstm_in[0]
    w_in_bf = w_in[:, :2 * hk + 2 * hv].astype(BF16)
    wg_bf = jnp.pad(w_in[:, 2 * hk + 2 * hv:], ((0, 0), (0, LANES - 2 * M_HEADS))).astype(BF16)
    bg_col = b_mlstm_gate[0].reshape(2 * M_HEADS, 1)
    wo1 = w_mlstm_out[0].astype(BF16)
    gain = row(mlstm_head_gain[0])

    qk_p, vv_p, oo_p, gt_p = _mlstm_in(hp1, row(mlstm_norm[0]), w_in_bf, wg_bf, bg_col, tm=512, act_dtype=BF16)
    hc_p, c_p, n_p, m_p = _mlstm_seq(qk_p, vv_p, oo_p, gt_p, gain, bp, sp)

    g1, wup1, cw1, cb1, wdn1 = ffn_w(1)
    y_p, conv_p1, ksum_b = _ffn(hp1, hc_p.reshape(mp, d), wo1, g1, wup1, cw1, cb1, wdn1, gf, tm=512, seq_len=sp,
                                final=True, pages=(page_table[half:], ck_t, nblk))

    k_s, v_s, qt_s, kt_s, vt_s = _qkv_step(xs, row(attn_norm[0]), wqkv)
    sel = jnp.concatenate([_select(qt_s, ksum_a, nblk, 0), _select(qt_s, ksum_b, nblk, half)], axis=0)
    sel = sel[:, :, :MOBA_TOPK].reshape(bs, N_HEADS * MOBA_TOPK)
    attn_s_t = _decode_attn(sel, page_table, ck_t, cv_t, qt_s, kt_s, vt_s, bias_s)
    attn_s = attn_s_t.T[:bs].astype(BF16)
    hs1, conv_s0 = _ffn(xs, attn_s, wo0, g0, wup0, cw0, cb0, wdn0, gf, tm=bs, seq_len=1, state=state_conv[0])

    qk_s, vv_s, oo_s, gt_s = _mlstm_in(hs1, row(mlstm_norm[0]), w_in_bf, wg_bf, bg_col, tm=bs, act_dtype=F32)
    hc_s, c_s, n_s, m_s = _mlstm_step(qk_s, vv_s, oo_s, gt_s.T, gain, state_C[0], state_n[0], state_m[0])
    y_s, conv_s1 = _ffn(hs1, hc_s.reshape(bs, d), wo1, g1, wup1, cw1, cb1, wdn1, gf, tm=bs, seq_len=1,
                        state=state_conv[1], final=True)

    kv5 = lambda a, n: a.reshape(1, n, -1, N_HEADS, HEAD_DIM)
    kv5t = lambda a: jnp.transpose(a.reshape(1, bp, N_HEADS, HEAD_DIM, sp), (0, 1, 4, 2, 3))
    return (y_p.reshape(bp, sp, d), y_s.reshape(bs, ts, d),
            kv5t(kt_p), kv5t(vt_p), kv5(k_s, bs), kv5(v_s, bs),
            c_p[None], n_p.reshape(1, bp, M_HEADS, M_QK_DIM), m_p[:, :, 0, 0][None],
            c_s[None], n_s.reshape(1, bs, M_HEADS, M_QK_DIM), m_s.reshape(1, bs, M_HEADS),
            jnp.stack([conv_p0, conv_p1]), jnp.stack([conv_s0, conv_s1]))
```

```python
import functools
import math

import numpy as np
import jax
import jax.numpy as jnp
from jax import lax
from jax.experimental import pallas as pl
from jax.experimental.pallas import tpu as pltpu

F32 = jnp.float32
BF16 = jnp.bfloat16

D_MODEL = 1024
N_HEADS = 16
HEAD_DIM = 64
MOBA_BLOCK = 256
MOBA_TOPK = 3
PAGE_SIZE = 128
REL_BUCKETS = 32
REL_MAX_DIST = 4096
M_HEADS = 4
M_QK_DIM = 128
M_V_DIM = 256
D_FF = 2816
EPS = 1e-6
NEG_INF = -1e30
LANES = 128
MLSTM_L = 256
VMEM_LIMIT = 56 * 1024 * 1024
FFN_VMEM_LIMIT = 62 * 1024 * 1024


def _cparams(sem, vmem=VMEM_LIMIT):
    return pltpu.CompilerParams(dimension_semantics=sem, vmem_limit_bytes=vmem)


def _dot(a, b):
    return jnp.dot(a, b, preferred_element_type=F32)


def _dot_nt(a, b):
    return lax.dot_general(a, b, (((1,), (1,)), ((), ())), preferred_element_type=F32)


def _dot_tn(a, b):
    return lax.dot_general(a, b, (((0,), (0,)), ((), ())), preferred_element_type=F32)


def _rmsnorm(x, g):
    ms = jnp.mean(x * x, axis=-1, keepdims=True)
    return x * lax.rsqrt(ms + EPS) * g


def _t5_bucket_np(dist):
    n = np.maximum(dist, 0).astype(np.int64)
    max_exact = REL_BUCKETS // 2
    nf = np.maximum(n, 1).astype(np.float64)
    large = max_exact + np.floor(
        np.log(nf / max_exact) / math.log(REL_MAX_DIST / max_exact) * (REL_BUCKETS - max_exact) + 1e-9
    ).astype(np.int64)
    large = np.minimum(large, REL_BUCKETS - 1)
    return np.where(n < max_exact, n, large).astype(np.int32)


def _qkv_seq_kernel(x_ref, g_ref, wq_ref, wkt_ref, wvt_ref, q_ref, kt_ref, vt_ref):
    xn = _rmsnorm(x_ref[...], g_ref[...]).astype(BF16)
    d = D_MODEL
    chunk = 512
    for c in range(0, d, chunk):
        q_ref[:, c:c + chunk] = (_dot(xn, wq_ref[:, c:c + chunk]) * (HEAD_DIM ** -0.5)).astype(q_ref.dtype)
    for c in range(0, d, chunk):
        kt_ref[0, c:c + chunk, :] = _dot_nt(wkt_ref[c:c + chunk, :], xn)
    for c in range(0, d, chunk):
        vt_ref[0, c:c + chunk, :] = _dot_nt(wvt_ref[c:c + chunk, :], xn)


def _qkv_seq(x, g, wq_bf, wkt_bf, wvt_bf, nseq, seq_len, tm):
    m, d = x.shape
    tps = seq_len // tm
    const = lambda i: (0, 0)
    return pl.pallas_call(
        _qkv_seq_kernel,
        grid=(m // tm,),
        in_specs=[
            pl.BlockSpec((tm, d), lambda i: (i, 0)),
            pl.BlockSpec((1, d), const),
            pl.BlockSpec((d, d), const),
            pl.BlockSpec((d, d), const),
            pl.BlockSpec((d, d), const),
        ],
        out_specs=[
            pl.BlockSpec((tm, d), lambda i: (i, 0)),
            pl.BlockSpec((1, d, tm), lambda i: (i // tps, 0, i % tps)),
            pl.BlockSpec((1, d, tm), lambda i: (i // tps, 0, i % tps)),
        ],
        out_shape=[
            jax.ShapeDtypeStruct((m, d), BF16),
            jax.ShapeDtypeStruct((nseq, d, seq_len), F32),
            jax.ShapeDtypeStruct((nseq, d, seq_len), F32),
        ],
        compiler_params=_cparams(("arbitrary",)),
        name="qkv_seq",
    )(x, g, wq_bf, wkt_bf, wvt_bf)


def _mlstm_in_kernel(x_ref, g_ref, w_ref, wg_ref, bg_ref, qk_ref, v_ref, o_ref, gt_ref, *, tm, pad_rows):
    xn = _rmsnorm(x_ref[...], g_ref[...]).astype(BF16)
    hk = M_HEADS * M_QK_DIM
    hv = M_HEADS * M_V_DIM
    qk_ref[:, 0:hk] = (_dot(xn, w_ref[:, 0:hk]) * (M_QK_DIM ** -0.5)).astype(qk_ref.dtype)
    qk_ref[:, hk:2 * hk] = _dot(xn, w_ref[:, hk:2 * hk]).astype(qk_ref.dtype)
    for c in range(0, hv, 512):
        v_ref[:, c:c + 512] = _dot(xn, w_ref[:, 2 * hk + c:2 * hk + c + 512]).astype(v_ref.dtype)
        o_ref[:, c:c + 512] = _dot(xn, w_ref[:, 2 * hk + hv + c:2 * hk + hv + c + 512])
    gates = _dot(xn, wg_ref[...])
    if pad_rows:
        gates = jnp.concatenate([gates, jnp.zeros((pad_rows, LANES), F32)], axis=0)
    gt = gates.T
    gt_ref[...] = gt[0:8, 0:tm] + bg_ref[...]


def _mlstm_in(x, g, w_bf, wg_bf, bg_col, tm, act_dtype):
    m, d = x.shape
    pad_rows = (-tm) % LANES
    kern = functools.partial(_mlstm_in_kernel, tm=tm, pad_rows=pad_rows)
    return pl.pallas_call(
        kern,
        grid=(m // tm,),
        in_specs=[
            pl.BlockSpec((tm, d), lambda i: (i, 0)),
            pl.BlockSpec((1, d), lambda i: (0, 0)),
            pl.BlockSpec(w_bf.shape, lambda i: (0, 0)),
            pl.BlockSpec(wg_bf.shape, lambda i: (0, 0)),
            pl.BlockSpec((8, 1), lambda i: (0, 0)),
        ],
        out_specs=[
            pl.BlockSpec((tm, 1024), lambda i: (i, 0)),
            pl.BlockSpec((tm, 1024), lambda i: (i, 0)),
            pl.BlockSpec((tm, 1024), lambda i: (i, 0)),
            pl.BlockSpec((8, tm), lambda i: (0, i)),
        ],
        out_shape=[
            jax.ShapeDtypeStruct((m, 1024), act_dtype),
            jax.ShapeDtypeStruct((m, 1024), act_dtype),
            jax.ShapeDtypeStruct((m, 1024), F32),
            jax.ShapeDtypeStruct((8, m), F32),
        ],
        compiler_params=_cparams(("arbitrary",)),
        name="mlstm_in",
    )(x, g, w_bf, wg_bf, bg_col)


def _bias_table_kernel(rb_ref, bk_ref, o_ref, *, bucket_sets):
    h = pl.program_id(0)
    c = bk_ref.shape[2]
    for t, buckets in enumerate(bucket_sets):
        bk = bk_ref[t]
        acc = jnp.zeros(bk.shape, F32)
        for bucket in buckets:
            acc = jnp.where(bk == bucket, rb_ref[bucket, h], acc)
        o_ref[0, :, t * c:(t + 1) * c] = acc


def _bias_tables(rel_bias, bucket_np, name):
    nt, r, c = bucket_np.shape
    bucket_sets = tuple(tuple(int(v) for v in np.unique(bucket_np[t])) for t in range(nt))
    kern = functools.partial(_bias_table_kernel, bucket_sets=bucket_sets)
    return pl.pallas_call(
        kern,
        grid=(N_HEADS,),
        in_specs=[
            pl.BlockSpec(memory_space=pltpu.SMEM),
            pl.BlockSpec((nt, r, c), lambda h: (0, 0, 0)),
        ],
        out_specs=pl.BlockSpec((1, r, nt * c), lambda h: (h, 0, 0)),
        out_shape=jax.ShapeDtypeStruct((N_HEADS, r, nt * c), F32),
        compiler_params=_cparams(("arbitrary",)),
        name=name,
    )(rel_bias, jnp.asarray(bucket_np))


def _moba_prompt_kernel(q_ref, kt_ref, vt_ref, bias_ref, o_ref, kaug, vbf, s_scr, p_scr, *, seq, nb):
    blk = MOBA_BLOCK
    ktf = kt_ref[0]
    kt_hi = ktf.astype(BF16)
    kt_lo = (ktf - kt_hi.astype(F32)).astype(BF16)
    col_blk = lax.broadcasted_iota(jnp.int32, (LANES, seq), 1) // blk
    row_s = lax.broadcasted_iota(jnp.int32, (LANES, seq), 0)
    kaug[0:LANES, :] = kt_hi
    kaug[LANES:2 * LANES, :] = jnp.where(col_blk == row_s, 1.0, 0.0).astype(BF16)
    vbf[...] = vt_ref[0].astype(BF16)
    mean_w = jnp.where(lax.broadcasted_iota(jnp.int32, (16, seq), 1) // blk
                       == lax.broadcasted_iota(jnp.int32, (16, seq), 0), 1.0 / blk, 0.0).astype(BF16)
    km = _dot_nt(mean_w, kt_hi) + _dot_nt(mean_w, kt_lo)
    km_hi = km.astype(BF16)
    km_lo = (km - km_hi.astype(F32)).astype(BF16)

    lane_q = lax.broadcasted_iota(jnp.int32, (blk, LANES), 1)
    head_lanes = [(lane_q // HEAD_DIM) == h for h in range(2)]
    cand = lax.broadcasted_iota(jnp.int32, (16, blk), 0)
    eye = (lax.broadcasted_iota(jnp.int32, (blk, blk), 0)
           == lax.broadcasted_iota(jnp.int32, (blk, blk), 1)).astype(BF16)
    causal = (lax.broadcasted_iota(jnp.int32, (blk, blk), 1)
              <= lax.broadcasted_iota(jnp.int32, (blk, blk), 0))

    items = [(ob, h) for ob in range(nb) for h in range(2)]
    n_slots = s_scr.shape[0]
    state = {}

    def stage_logits(t):
        ob, h = items[t]
        slot = t % n_slots
        q0 = ob * blk
        qb = q_ref[0, q0:q0 + blk, :]
        qm = jnp.where(head_lanes[h], qb, jnp.zeros_like(qb))
        if ob <= MOBA_TOPK:
            lhs, k_rows = qm, LANES
        else:
            sc = _dot_nt(km_hi, qm) + _dot_nt(km_lo, qm)
            cnt = jnp.zeros((16, blk), F32)
            for m in range(ob):
                row = sc[m:m + 1, :]
                cnt = cnt + jnp.where(row > sc, 1.0, jnp.where((row == sc) & (cand > m), 1.0, 0.0))
            pen_t = jnp.where((cnt < MOBA_TOPK) | (cand >= ob), 0.0, NEG_INF).astype(BF16)
            pen_t = jnp.concatenate([pen_t, jnp.zeros((LANES - 16, blk), BF16)], axis=0)
            pen = _dot_nt(eye, pen_t).astype(BF16)
            lhs, k_rows = jnp.concatenate([qm, pen], axis=1), 2 * LANES
        m_part = None
        for n in range(ob + 1):
            cols = slice(n * blk, (n + 1) * blk)
            s_n = _dot(lhs, kaug[0:k_rows, cols]) + bias_ref[h, :, (nb - 1 - ob + n) * blk:(nb - ob + n) * blk]
            if n == ob:
                s_n = jnp.where(causal, s_n, NEG_INF)
            s_scr[slot, :, cols] = s_n
            t_max = jnp.maximum(s_n[:, 0:LANES], s_n[:, LANES:])
            m_part = t_max if m_part is None else jnp.maximum(m_part, t_max)
            if n < ob:
                yield
        state[t] = {"m": jnp.max(m_part, axis=-1, keepdims=True)}
        yield

    def stage_softmax(t):
        ob, h = items[t]
        slot = t % n_slots
        m_i = state[t]["m"]
        l_part = None
        for n in range(ob + 1):
            cols = slice(n * blk, (n + 1) * blk)
            p_n = jnp.exp(s_scr[slot, :, cols] - m_i)
            p_scr[slot, :, cols] = p_n.astype(p_scr.dtype)
            t_sum = p_n[:, 0:LANES] + p_n[:, LANES:]
            l_part = t_sum if l_part is None else l_part + t_sum
            if n < ob:
                yield
        state[t]["l"] = jnp.sum(l_part, axis=-1, keepdims=True)
        yield

    def stage_pv(t):
        ob, h = items[t]
        slot = t % n_slots
        acc = None
        for n in range(ob + 1):
            cols = slice(n * blk, (n + 1) * blk)
            part = _dot_nt(vbf[:, cols], p_scr[slot, :, cols])
            acc = part if acc is None else acc + part
            if n < ob:
                yield
        out = acc.T / state[t]["l"]
        if h == 0:
            state[t]["out"] = out
        else:
            q0 = ob * blk
            o_ref[0, q0:q0 + blk, :] = jnp.where(head_lanes[0], state[t - 1]["out"], out).astype(o_ref.dtype)
            del state[t - 1], state[t]
        yield

    for t in range(len(items) + 2):
        gens = []
        if t < len(items):
            gens.append(stage_logits(t))
        if 0 <= t - 1 < len(items):
            gens.append(stage_softmax(t - 1))
        if 0 <= t - 2 < len(items):
            gens.append(stage_pv(t - 2))
        while gens:
            for gen in list(gens):
                if next(gen, "done") == "done":
                    gens.remove(gen)


def _moba_prompt(q, kt, vt, bias_strip):
    b, s, d = q.shape
    nb = s // MOBA_BLOCK
    hp = d // LANES
    kern = functools.partial(_moba_prompt_kernel, seq=s, nb=nb)
    return pl.pallas_call(
        kern,
        grid=(hp, b),
        in_specs=[
            pl.BlockSpec((1, s, LANES), lambda p, i: (i, 0, p)),
            pl.BlockSpec((1, LANES, s), lambda p, i: (i, p, 0)),
            pl.BlockSpec((1, LANES, s), lambda p, i: (i, p, 0)),
            pl.BlockSpec((2, MOBA_BLOCK, s), lambda p, i: (p, 0, 0)),
        ],
        out_specs=pl.BlockSpec((1, s, LANES), lambda p, i: (i, 0, p)),
        out_shape=jax.ShapeDtypeStruct((b, s, d), BF16),
        scratch_shapes=[pltpu.VMEM((2 * LANES, s), BF16), pltpu.VMEM((LANES, s), BF16),
                        pltpu.VMEM((4, MOBA_BLOCK, s), F32), pltpu.VMEM((4, MOBA_BLOCK, s), BF16)],
        compiler_params=_cparams(("arbitrary", "arbitrary")),
        name="moba_prompt",
    )(q, kt, vt, bias_strip)


class _KmPlan:
    def __init__(self, n_steps, n_seq, nblk):
        total = n_seq * nblk
        assert total % n_steps == 0
        self.n_seq, self.nblk, self.total = n_seq, nblk, total
        self.per_step = total // n_steps
        assert nblk % self.per_step == 0
        self.steps_per_seq = nblk // self.per_step
        self.group = next(g for g in (4, 2, 1) if self.per_step % g == 0)
        self.points = self.per_step // self.group
        self.ahead = 3 * self.group
        self.nbuf = self.ahead + self.group


def _ffn_kernel(*refs, tm, tiles_per_seq, seq_mode, final, km=None):
    if km is not None:
        pt_ref, refs = refs[0], refs[1:]
        (x_ref, a_ref, wo_ref, g_ref, wup_ref, cw_ref, cb_ref, wdn_ref, gf_ref, ck_ref,
         y_ref, cs_ref, km_ref, hbuf, carry, kbuf, ksem) = refs
    elif seq_mode:
        (x_ref, a_ref, wo_ref, g_ref, wup_ref, cw_ref, cb_ref, wdn_ref, gf_ref,
         y_ref, cs_ref, hbuf, carry) = refs
    else:
        (x_ref, a_ref, wo_ref, g_ref, wup_ref, cw_ref, cb_ref, wdn_ref, gf_ref, st0_ref, st1_ref,
         y_ref, cs0_ref, cs1_ref, hbuf) = refs
    i = pl.program_id(0)
    cf = 256
    n_chunks = D_FF // cf

    if km is not None:
        ppb = MOBA_BLOCK // PAGE_SIZE
        lane_km = lax.broadcasted_iota(jnp.int32, (N_HEADS * HEAD_DIM, LANES), 1)

        def km_copies(g):
            slot = g % km.nbuf
            sq = g // km.nblk
            n = g % km.nblk
            return [pltpu.make_async_copy(ck_ref.at[pt_ref[sq, ppb * n + j]], kbuf.at[slot, j], ksem.at[slot, j])
                    for j in range(ppb)]

        @pl.when(i == 0)
        def _():
            for g in range(km.ahead):
                for j, cp in enumerate(km_copies(g)):
                    cp.start(priority=j % 2)

        @pl.when(i % km.steps_per_seq == 0)
        def _():
            km_ref[...] = jnp.zeros_like(km_ref)

        def km_wait(p):
            g0 = i * km.per_step + p * km.group
            for u in range(km.group):
                for cp in km_copies(g0 + u):
                    cp.wait()
            for u in range(km.group):
                @pl.when(g0 + km.ahead + u < km.total)
                def _():
                    for j, cp in enumerate(km_copies(g0 + km.ahead + u)):
                        cp.start(priority=j % 2)

        def km_reduce(p):
            g0 = i * km.per_step + p * km.group
            acc = km_ref[0]
            for u in range(km.group):
                slot = (g0 + u) % km.nbuf
                x = kbuf[slot, 0]
                for j in range(1, ppb):
                    x = x + kbuf[slot, j]
                ssum = jnp.sum(x.reshape(N_HEADS * HEAD_DIM, PAGE_SIZE), axis=-1, keepdims=True)
                acc = jnp.where(lane_km == (g0 + u) % km.nblk, ssum, acc)
            km_ref[0] = acc

        km_at = {}
        for p in range(km.points):
            km_at.setdefault((p * n_chunks) // km.points, []).append(p)

    x1 = x_ref[...] + _dot(a_ref[...], wo_ref[...])
    xn = _rmsnorm(x1, g_ref[...]).astype(BF16)
    if seq_mode:
        @pl.when(i % tiles_per_seq == 0)
        def _():
            carry[...] = jnp.zeros_like(carry)
        row = lax.broadcasted_iota(jnp.int32, (tm, 256), 0)
    for ci, c in enumerate(range(0, D_FF, cf)):
        if km is not None:
            for p in km_at.get(ci, ()):
                if p > 0:
                    km_reduce(p - 1)
                km_wait(p)
        ug = _dot(xn, wup_ref[:, c:c + cf])
        uv = _dot(xn, wup_ref[:, D_FF + c:D_FF + c + cf])
        if seq_mode:
            c0 = carry[0:1, c:c + cf]
            c1 = carry[1:2, c:c + cf]
            p1 = jnp.where(row == 0, c1, pltpu.roll(ug, 1, axis=0))
            p2 = jnp.where(row == 0, c0, jnp.where(row == 1, c1, pltpu.roll(ug, 2, axis=0)))
            last2 = ug[tm - 2:tm, :]
            carry[0:2, c:c + cf] = last2
            cs_ref[0, :, c:c + cf] = last2
        else:
            p2 = st0_ref[:, c:c + cf]
            p1 = st1_ref[:, c:c + cf]
            cs0_ref[:, c:c + cf] = p1
            cs1_ref[:, c:c + cf] = ug
        conv = (cb_ref[:, c:c + cf] + cw_ref[0:1, c:c + cf] * p2 + cw_ref[1:2, c:c + cf] * p1
                + cw_ref[2:3, c:c + cf] * ug)
        hbuf[:, c:c + cf] = (conv * (1.0 / (1.0 + jnp.exp(-conv))) * uv).astype(hbuf.dtype)
    if km is not None:
        km_reduce(km.points - 1)
    y = x1 + _dot(hbuf[...], wdn_ref[...])
    if final:
        y = _rmsnorm(y, gf_ref[...])
    y_ref[...] = y


def _ffn(x, a, wo_bf, g, wup_bf, cw, cb, wdn_bf, gf, *, tm, seq_len, state=None, final=False, pages=None):
    m, d = x.shape
    seq_mode = state is None
    tiles_per_seq = (seq_len // tm) if seq_mode else 1
    n_steps = m // tm
    km = None
    if pages is not None:
        pt_rows, ck_t, nblk = pages
        km = _KmPlan(n_steps, pt_rows.shape[0], nblk)
    kern = functools.partial(_ffn_kernel, tm=tm, tiles_per_seq=tiles_per_seq, seq_mode=seq_mode, final=final,
                             km=km)
    const = lambda i, *_: (0, 0)
    rows = lambda i, *_: (i, 0)
    resident = functools.partial(pl.BlockSpec, pipeline_mode=pl.Buffered(1))
    in_specs = [
        pl.BlockSpec((tm, d), rows),
        pl.BlockSpec((tm, d), rows),
        resident(wo_bf.shape, const),
        pl.BlockSpec((1, d), const),
        resident(wup_bf.shape, const),
        pl.BlockSpec((3, D_FF), const),
        pl.BlockSpec((1, D_FF), const),
        resident(wdn_bf.shape, const),
        pl.BlockSpec((1, d), const),
    ]
    args = [x, a, wo_bf, g, wup_bf, cw, cb, wdn_bf, gf]
    scratch = [pltpu.VMEM((tm, D_FF), BF16)]
    if seq_mode:
        nseq = m // seq_len
        cs_shapes = [jax.ShapeDtypeStruct((nseq, 2, D_FF), F32)]
        cs_specs = [pl.BlockSpec((1, 2, D_FF), lambda i, *_: (i // tiles_per_seq, 0, 0))]
        scratch.append(pltpu.VMEM((8, D_FF), F32))
    else:
        in_specs += [pl.BlockSpec((tm, D_FF), rows)] * 2
        args += [state[:, 0], state[:, 1]]
        cs_shapes = [jax.ShapeDtypeStruct((m, D_FF), F32)] * 2
        cs_specs = [pl.BlockSpec((tm, D_FF), rows)] * 2
    n_prefetch = 0
    if km is not None:
        assert seq_mode
        ppb = MOBA_BLOCK // PAGE_SIZE
        n_prefetch = 1
        in_specs.append(pl.BlockSpec(memory_space=pl.ANY))
        args = [pt_rows] + args + [ck_t]
        cs_shapes.append(jax.ShapeDtypeStruct((km.n_seq, N_HEADS * HEAD_DIM, LANES), F32))
        cs_specs.append(pl.BlockSpec((1, N_HEADS * HEAD_DIM, LANES),
                                     lambda i, *_: (i // km.steps_per_seq, 0, 0)))
        scratch += [pltpu.VMEM((km.nbuf, ppb, N_HEADS, HEAD_DIM, PAGE_SIZE), F32),
                    pltpu.SemaphoreType.DMA((km.nbuf, ppb))]
    outs = pl.pallas_call(
        kern,
        grid_spec=pltpu.PrefetchScalarGridSpec(
            num_scalar_prefetch=n_prefetch,
            grid=(n_steps,),
            in_specs=in_specs,
            out_specs=[pl.BlockSpec((tm, d), rows)] + cs_specs,
            scratch_shapes=scratch,
        ),
        out_shape=[jax.ShapeDtypeStruct((m, d), F32)] + cs_shapes,
        compiler_params=_cparams(("arbitrary",), vmem=FFN_VMEM_LIMIT if km is not None else VMEM_LIMIT),
        name="ffn_seq" if seq_mode else "ffn_step",
    )(*args)
    if not seq_mode:
        return outs[0], jnp.stack([outs[1], outs[2]], axis=1)
    return tuple(outs)


def _log_sigmoid(x):
    return jnp.minimum(x, 0.0) - jnp.log1p(jnp.exp(-jnp.abs(x)))


def _mlstm_seq_kernel(q_ref, k_ref, v_ref, o_ref, gt_ref, gain_ref,
                      h_ref, c_out, n_out, m_out, gate_s, *, seq):
    L = MLSTM_L
    hd = pl.program_id(1)
    tri = (lax.broadcasted_iota(jnp.int32, (L, L), 1) <= lax.broadcasted_iota(jnp.int32, (L, L), 0))
    diag = (lax.broadcasted_iota(jnp.int32, (L, L), 1) == lax.broadcasted_iota(jnp.int32, (L, L), 0))

    ig_all = gt_ref[pl.ds(hd, 1), :]
    lf_all = _log_sigmoid(gt_ref[pl.ds(M_HEADS + hd, 1), :])
    pos = lax.broadcasted_iota(jnp.int32, (1, seq), 1) % L
    bc_all = lf_all
    sh = 1
    while sh < L:
        bc_all = bc_all + jnp.where(pos >= sh, pltpu.roll(bc_all, sh, axis=1), 0.0)
        sh *= 2
    gate_s[0:1, :] = ig_all
    gate_s[1:2, :] = lf_all
    gate_s[2:3, :] = ig_all - bc_all

    c_st = jnp.zeros((M_QK_DIM, M_V_DIM), F32)
    n_st = jnp.zeros((1, M_QK_DIM), F32)
    m_prev = jnp.zeros((1, 1), F32)
    for ci in range(seq // L):
        t0 = ci * L
        ig = gate_s[0:1, t0:t0 + L]
        lf = gate_s[1:2, t0:t0 + L]
        u = gate_s[2:3, t0:t0 + L]
        cmax_c = jnp.max(jnp.where(tri, u, -jnp.inf), axis=-1, keepdims=True)
        bc_c = jnp.sum(jnp.where(tri, lf, 0.0), axis=-1, keepdims=True)
        ig_c = jnp.sum(jnp.where(diag, ig, 0.0), axis=-1, keepdims=True)
        big_m = jnp.maximum(m_prev, cmax_c)
        dmat = jnp.exp(jnp.where(tri, u - big_m, -jnp.inf))
        inter = jnp.exp(m_prev - big_m)
        mt_c = bc_c + big_m

        q = q_ref[0, t0:t0 + L, :]
        k = k_ref[0, t0:t0 + L, :]
        v = v_ref[0, t0:t0 + L, :]
        sm = _dot_nt(q, k) * dmat
        num = _dot(sm.astype(BF16), v) + inter * _dot(q, c_st.astype(BF16))
        qn = jnp.sum(q.astype(F32) * n_st, axis=-1, keepdims=True)
        den = jnp.sum(sm, axis=-1, keepdims=True) + inter * qn
        hout = num / jnp.maximum(jnp.abs(den), jnp.exp(-mt_c))

        bc_last = jnp.sum(lf, axis=-1, keepdims=True)
        m_new = bc_last + jnp.maximum(m_prev, jnp.max(u, axis=-1, keepdims=True))
        decay = jnp.exp(bc_last + m_prev - m_new)
        w_r = jnp.exp(u + bc_last - m_new)
        w_c = jnp.exp(ig_c - bc_c + bc_last - m_new)
        vw = (v.astype(F32) * w_c).astype(BF16)
        c_st = decay * c_st + _dot_tn(k, vw)
        w8 = jnp.broadcast_to(w_r, (8, L)).astype(BF16)
        n_st = decay * n_st + _dot(w8, k)[0:1, :]
        m_prev = m_new

        hn = hout * lax.rsqrt(jnp.mean(hout * hout, axis=-1, keepdims=True) + EPS)
        o = o_ref[0, t0:t0 + L, :]
        hn = hn * gain_ref[...] * (1.0 / (1.0 + jnp.exp(-o)))
        h_ref[0, t0:t0 + L, :] = hn.astype(h_ref.dtype)

    c_out[0, 0] = c_st
    n_out[0, 0] = n_st
    m_out[0, 0] = jnp.broadcast_to(m_prev, (1, LANES))


def _mlstm_seq(qk, v, o, gt, gain, b, s):
    kern = functools.partial(_mlstm_seq_kernel, seq=s)
    qk3 = qk.reshape(b, s, 2 * M_HEADS * M_QK_DIM)
    v3 = v.reshape(b, s, M_HEADS * M_V_DIM)
    o3 = o.reshape(b, s, M_HEADS * M_V_DIM)
    return pl.pallas_call(
        kern,
        grid=(b, M_HEADS),
        in_specs=[
            pl.BlockSpec((1, s, M_QK_DIM), lambda i, h: (i, 0, h)),
            pl.BlockSpec((1, s, M_QK_DIM), lambda i, h: (i, 0, M_HEADS + h)),
            pl.BlockSpec((1, s, M_V_DIM), lambda i, h: (i, 0, h)),
            pl.BlockSpec((1, s, M_V_DIM), lambda i, h: (i, 0, h)),
            pl.BlockSpec((8, s), lambda i, h: (0, i)),
            pl.BlockSpec((1, M_V_DIM), lambda i, h: (0, h)),
        ],
        out_specs=[
            pl.BlockSpec((1, s, M_V_DIM), lambda i, h: (i, 0, h)),
            pl.BlockSpec((1, 1, M_QK_DIM, M_V_DIM), lambda i, h: (i, h, 0, 0)),
            pl.BlockSpec((1, 1, 1, M_QK_DIM), lambda i, h: (i, h, 0, 0)),
            pl.BlockSpec((1, 1, 1, LANES), lambda i, h: (i, h, 0, 0)),
        ],
        out_shape=[
            jax.ShapeDtypeStruct((b, s, M_HEADS * M_V_DIM), BF16),
            jax.ShapeDtypeStruct((b, M_HEADS, M_QK_DIM, M_V_DIM), F32),
            jax.ShapeDtypeStruct((b, M_HEADS, 1, M_QK_DIM), F32),
            jax.ShapeDtypeStruct((b, M_HEADS, 1, LANES), F32),
        ],
        scratch_shapes=[pltpu.VMEM((8, s), F32)],
        compiler_params=_cparams(("arbitrary", "arbitrary")),
        name="mlstm_seq",
    )(qk3, qk3, v3, o3, gt, gain)


def _mlstm_step_kernel(qk_ref, v_ref, o_ref, g_ref, gain_ref, c_ref, n_ref, m_ref,
                       h_ref, c_out, n_out, m_out):
    eye = (lax.broadcasted_iota(jnp.int32, (M_QK_DIM, M_QK_DIM), 0)
           == lax.broadcasted_iota(jnp.int32, (M_QK_DIM, M_QK_DIM), 1))
    hk = M_HEADS * M_QK_DIM
    for h in range(M_HEADS):
        q = qk_ref[0, :, h * M_QK_DIM:(h + 1) * M_QK_DIM]
        k = qk_ref[0, :, hk + h * M_QK_DIM:hk + (h + 1) * M_QK_DIM]
        v = v_ref[0, :, h * M_V_DIM:(h + 1) * M_V_DIM]
        o = o_ref[0, :, h * M_V_DIM:(h + 1) * M_V_DIM]
        ig = g_ref[0, :, h:h + 1]
        lf = _log_sigmoid(g_ref[0, :, M_HEADS + h:M_HEADS + h + 1])
        m_prev = m_ref[0, :, h:h + 1]
        c = c_ref[0, h]
        n = n_ref[0, h]
        q_col = jnp.sum(jnp.where(eye, q, 0.0), axis=-1, keepdims=True)
        k_col = jnp.sum(jnp.where(eye, k, 0.0), axis=-1, keepdims=True)
        g = lf + m_prev
        mt = jnp.maximum(g, ig)
        dm = jnp.exp(ig - mt)
        inter = jnp.exp(g - mt)
        sm = jnp.sum(q * k, axis=-1, keepdims=True) * dm
        qc = jnp.sum(q_col * c, axis=0, keepdims=True)
        num = sm * v + inter * qc
        den = sm + inter * jnp.sum(q * n, axis=-1, keepdims=True)
        hout = num / jnp.maximum(jnp.abs(den), jnp.exp(-mt))
        c_out[0, h] = inter * c + dm * (k_col * v)
        n_out[0, h] = inter * n + dm * k
        m_out[0, :, h:h + 1] = mt
        hn = hout * lax.rsqrt(jnp.mean(hout * hout, axis=-1, keepdims=True) + EPS)
        hn = hn * gain_ref[:, h * M_V_DIM:(h + 1) * M_V_DIM] * (1.0 / (1.0 + jnp.exp(-o)))
        h_ref[0, :, h * M_V_DIM:(h + 1) * M_V_DIM] = hn.astype(h_ref.dtype)


def _mlstm_step(qk, v, o, g_rows, gain, c0, n0, m0):
    nb = qk.shape[0]
    hv = M_HEADS * M_V_DIM
    row3 = lambda a: a.reshape(nb, 1, a.shape[-1])
    spec3 = lambda w: pl.BlockSpec((1, 1, w), lambda i: (i, 0, 0))
    return pl.pallas_call(
        _mlstm_step_kernel,
        grid=(nb,),
        in_specs=[
            spec3(1024), spec3(hv), spec3(hv), spec3(8),
            pl.BlockSpec((1, hv), lambda i: (0, 0)),
            pl.BlockSpec((1, M_HEADS, M_QK_DIM, M_V_DIM), lambda i: (i, 0, 0, 0)),
            pl.BlockSpec((1, M_HEADS, 1, M_QK_DIM), lambda i: (i, 0, 0, 0)),
            spec3(M_HEADS),
        ],
        out_specs=[
            spec3(hv),
            pl.BlockSpec((1, M_HEADS, M_QK_DIM, M_V_DIM), lambda i: (i, 0, 0, 0)),
            pl.BlockSpec((1, M_HEADS, 1, M_QK_DIM), lambda i: (i, 0, 0, 0)),
            spec3(M_HEADS),
        ],
        out_shape=[
            jax.ShapeDtypeStruct((nb, 1, hv), BF16),
            jax.ShapeDtypeStruct((nb, M_HEADS, M_QK_DIM, M_V_DIM), F32),
            jax.ShapeDtypeStruct((nb, M_HEADS, 1, M_QK_DIM), F32),
            jax.ShapeDtypeStruct((nb, 1, M_HEADS), F32),
        ],
        compiler_params=_cparams(("arbitrary",)),
        name="mlstm_step",
    )(row3(qk), row3(v), row3(o), row3(g_rows), gain, c0,
      n0.reshape(nb, M_HEADS, 1, M_QK_DIM), row3(m0))


def _select_kernel(qt_ref, km_ref, sel_ref, *, nblk, seq_base):
    b = pl.program_id(0) + seq_base
    lane_q = lax.broadcasted_iota(jnp.int32, (N_HEADS * HEAD_DIM, LANES), 1)
    q_col = jnp.sum(jnp.where(lane_q == b, qt_ref[...], 0.0), axis=-1, keepdims=True)
    prod = (km_ref[0] * q_col).reshape(N_HEADS, HEAD_DIM, LANES)
    sc = jnp.sum(prod, axis=1) * (1.0 / MOBA_BLOCK)
    lane = lax.broadcasted_iota(jnp.int32, (N_HEADS, LANES), 1)
    lane_f = lane.astype(F32)
    sc = jnp.where(lane < nblk, sc, -jnp.inf)
    out = jnp.zeros((N_HEADS, LANES), jnp.int32)
    for r in range(MOBA_TOPK):
        mx = jnp.max(sc, axis=-1, keepdims=True)
        idx = jnp.min(jnp.where(sc == mx, lane_f, float(LANES)), axis=-1, keepdims=True)
        out = jnp.where(lane == r, idx.astype(jnp.int32), out)
        sc = jnp.where(lane_f == idx, -jnp.inf, sc)
    sel_ref[0] = out


def _select(qt, kmean_t, nblk, seq_base):
    n_seq = kmean_t.shape[0]
    kern = functools.partial(_select_kernel, nblk=nblk, seq_base=seq_base)
    return pl.pallas_call(
        kern,
        grid=(n_seq,),
        in_specs=[
            pl.BlockSpec(qt.shape, lambda b: (0, 0)),
            pl.BlockSpec((1, N_HEADS * HEAD_DIM, LANES), lambda b: (b, 0, 0)),
        ],
        out_specs=pl.BlockSpec((1, N_HEADS, LANES), lambda b: (b, 0, 0)),
        out_shape=jax.ShapeDtypeStruct((n_seq, N_HEADS, LANES), jnp.int32),
        compiler_params=_cparams(("arbitrary",)),
        name="moba_select",
    )(qt, kmean_t)


def _decode_attn_kernel(sel_ref, pt_ref, ck_ref, cv_ref, qt_ref, kt_ref, vt_ref, bias_ref, o_ref,
                        kbuf, vbuf, sem, *, n_pages, n_seq):
    b = pl.program_id(0)
    ppb = MOBA_BLOCK // PAGE_SIZE
    n_t = MOBA_TOPK * ppb
    self_row = n_pages
    slot = b % 2

    def copies(bb, sl, h, t):
        r, j = divmod(t, ppb)
        lp = ppb * sel_ref[bb, h * MOBA_TOPK + r] + j
        page = pt_ref[bb, lp]
        return (pltpu.make_async_copy(ck_ref.at[page, h], kbuf.at[sl, h, t], sem.at[sl, 0, h, t]),
                pltpu.make_async_copy(cv_ref.at[page, h], vbuf.at[sl, h, t], sem.at[sl, 1, h, t]))

    def start_all(bb, sl):
        def per_head(h, carry):
            for t in range(n_t):
                for kind, cp in enumerate(copies(bb, sl, h, t)):
                    cp.start(priority=kind)
            return carry
        lax.fori_loop(0, N_HEADS, per_head, 0)

    @pl.when(b == 0)
    def _():
        o_ref[...] = jnp.zeros_like(o_ref)
        start_all(0, 0)

    @pl.when(b + 1 < n_seq)
    def _():
        start_all(b + 1, 1 - slot)

    def wait_head(h, carry):
        for t in range(n_t):
            for cp in copies(b, slot, h, t):
                cp.wait()
        return carry
    lax.fori_loop(0, N_HEADS, wait_head, 0)

    lane_b = lax.broadcasted_iota(jnp.int32, (N_HEADS * HEAD_DIM, LANES), 1) == b
    pick = lambda ref: jnp.sum(jnp.where(lane_b, ref[...], 0.0), axis=-1, keepdims=True)
    q_col = pick(qt_ref)
    k_col = pick(kt_ref)
    v_col = pick(vt_ref)
    lane_o = lax.broadcasted_iota(jnp.int32, (HEAD_DIM, LANES), 1) == b

    head_rows = lambda h: slice(h * HEAD_DIM, (h + 1) * HEAD_DIM)
    s_tiles = []
    for t in range(n_t):
        r, j = divmod(t, ppb)
        per_head = []
        for h in range(N_HEADS):
            lp = ppb * sel_ref[b, h * MOBA_TOPK + r] + j
            per_head.append(jnp.sum(kbuf[slot, h, t] * q_col[head_rows(h)], axis=0, keepdims=True)
                            + bias_ref[h, pl.ds(lp, 1), :])
        s_tiles.append(jnp.concatenate(per_head, axis=0))
    s_self = (jnp.sum((q_col * k_col).reshape(N_HEADS, HEAD_DIM, 1), axis=1)
              + bias_ref[:, self_row:self_row + 1, 0:1].reshape(N_HEADS, 1))
    m = s_self
    for s_t in s_tiles:
        m = jnp.maximum(m, jnp.max(s_t, axis=-1, keepdims=True))
    p_self = jnp.exp(s_self - m)
    p_tiles = [jnp.exp(s_t - m) for s_t in s_tiles]
    l = p_self
    for p_t in p_tiles:
        l = l + jnp.sum(p_t, axis=-1, keepdims=True)
    for h in range(N_HEADS):
        acc = vbuf[slot, h, 0] * p_tiles[0][h:h + 1, :]
        for t in range(1, n_t):
            acc = acc + vbuf[slot, h, t] * p_tiles[t][h:h + 1, :]
        o_col = ((jnp.sum(acc, axis=-1, keepdims=True) + p_self[h:h + 1, :] * v_col[head_rows(h)])
                 / l[h:h + 1, :])
        o_ref[head_rows(h), :] = jnp.where(lane_o, o_col, o_ref[head_rows(h), :])


def _decode_attn(sel, page_table, ck_t, cv_t, qt, kt, vt, bias_s):
    n_seq, n_pages = page_table.shape
    kern = functools.partial(_decode_attn_kernel, n_pages=n_pages, n_seq=n_seq)
    n_t = MOBA_TOPK * (MOBA_BLOCK // PAGE_SIZE)
    full = lambda a: pl.BlockSpec(a.shape, lambda b, s, p: (0,) * a.ndim)
    return pl.pallas_call(
        kern,
        grid_spec=pltpu.PrefetchScalarGridSpec(
            num_scalar_prefetch=2,
            grid=(n_seq,),
            in_specs=[pl.BlockSpec(memory_space=pl.ANY), pl.BlockSpec(memory_space=pl.ANY),
                      full(qt), full(kt), full(vt), full(bias_s)],
            out_specs=pl.BlockSpec((N_HEADS * HEAD_DIM, LANES), lambda b, s, p: (0, 0)),
            scratch_shapes=[pltpu.VMEM((2, N_HEADS, n_t, HEAD_DIM, PAGE_SIZE), F32),
                            pltpu.VMEM((2, N_HEADS, n_t, HEAD_DIM, PAGE_SIZE), F32),
                            pltpu.SemaphoreType.DMA((2, 2, N_HEADS, n_t))],
        ),
        out_shape=jax.ShapeDtypeStruct((N_HEADS * HEAD_DIM, LANES), F32),
        compiler_params=_cparams(("arbitrary",)),
        name="moba_decode",
    )(sel, page_table, ck_t, cv_t, qt, kt, vt, bias_s)


def _qkv_step_kernel(x_ref, g_ref, w_ref, k_ref, v_ref, qt_ref, kt_ref, vt_ref, *, nb):
    xn = _rmsnorm(x_ref[...], g_ref[...]).astype(BF16)
    d = D_MODEL
    pad = jnp.zeros((LANES - nb, d), F32)
    q = _dot(xn, w_ref[:, 0:d]) * (HEAD_DIM ** -0.5)
    k = _dot(xn, w_ref[:, d:2 * d])
    v = _dot(xn, w_ref[:, 2 * d:3 * d])
    k_ref[...] = k
    v_ref[...] = v
    qt_ref[...] = jnp.concatenate([q, pad], axis=0).T
    kt_ref[...] = jnp.concatenate([k, pad], axis=0).T
    vt_ref[...] = jnp.concatenate([v, pad], axis=0).T


def _qkv_step(x, g, w_bf):
    nb, d = x.shape
    kern = functools.partial(_qkv_step_kernel, nb=nb)
    full = lambda shape: pl.BlockSpec(shape, lambda i: (0,) * len(shape))
    return pl.pallas_call(
        kern,
        grid=(1,),
        in_specs=[full((nb, d)), full((1, d)), full(w_bf.shape)],
        out_specs=[full((nb, d)), full((nb, d)), full((d, LANES)), full((d, LANES)), full((d, LANES))],
        out_shape=[jax.ShapeDtypeStruct((nb, d), F32), jax.ShapeDtypeStruct((nb, d), F32),
                   jax.ShapeDtypeStruct((d, LANES), F32), jax.ShapeDtypeStruct((d, LANES), F32),
                   jax.ShapeDtypeStruct((d, LANES), F32)],
        compiler_params=_cparams(("arbitrary",)),
        name="qkv_step",
    )(x, g, w_bf)


def kernel(x_prompt, x_sample, cache_k, cache_v, state_C, state_n, state_m, state_conv, page_table, rel_bias, attn_norm, w_qkv, w_attn_out, mlstm_norm, w_mlstm_in, b_mlstm_gate, mlstm_head_gain, w_mlstm_out, ffn_norm, w_ffn_up, ffn_conv_w, ffn_conv_b, w_ffn_down, final_norm):
    bp, sp, d = x_prompt.shape
    bs, ts, _ = x_sample.shape
    assert ts == 1 and d == D_MODEL and sp % MOBA_BLOCK == 0 and sp % MLSTM_L == 0
    n_pages = page_table.shape[1]
    assert (n_pages * PAGE_SIZE) % MOBA_BLOCK == 0
    nblk = n_pages * PAGE_SIZE // MOBA_BLOCK
    assert MOBA_TOPK <= nblk <= LANES and bs % 2 == 0
    mp = bp * sp
    nb_p = sp // MOBA_BLOCK
    assert nb_p <= 16
    hk = M_HEADS * M_QK_DIM
    hv = M_HEADS * M_V_DIM
    row = lambda a: a.reshape(1, -1)

    ii = np.arange(MOBA_BLOCK)
    bk_prompt = np.stack([_t5_bucket_np((nb_p - 1 - t) * MOBA_BLOCK + ii[:, None] - ii[None, :])
                          for t in range(nb_p)])
    past = n_pages * PAGE_SIZE
    kpos = np.arange(n_pages * PAGE_SIZE).reshape(n_pages, PAGE_SIZE)
    bk_step = np.concatenate([_t5_bucket_np(past - kpos), np.zeros((8, PAGE_SIZE), np.int32)])[None]
    bias_p = _bias_tables(rel_bias, bk_prompt, "bias_prompt")
    bias_s = _bias_tables(rel_bias, bk_step, "bias_step")

    xp = x_prompt.reshape(mp, d)
    xs = x_sample.reshape(bs, d)

    wqkv = w_qkv[0].astype(BF16)
    wo0 = w_attn_out[0].astype(BF16)
    q_p, kt_p, vt_p = _qkv_seq(xp, row(attn_norm[0]), wqkv[:, :d], wqkv[:, d:2 * d].T, wqkv[:, 2 * d:].T,
                               bp, sp, tm=512)
    attn_p = _moba_prompt(q_p.reshape(bp, sp, d), kt_p, vt_p, bias_p)

    ck_t = jnp.transpose(cache_k[0], (0, 2, 3, 1))
    cv_t = jnp.transpose(cache_v[0], (0, 2, 3, 1))
    half = bs // 2

    ffn_w = lambda i: (row(ffn_norm[i]), w_ffn_up[i].astype(BF16), ffn_conv_w[i], row(ffn_conv_b[i]),
                       w_ffn_down[i].astype(BF16))
    g0, wup0, cw0, cb0, wdn0 = ffn_w(0)
    gf = row(final_norm)
    hp1, conv_p0, ksum_a = _ffn(xp, attn_p.reshape(mp, d), wo0, g0, wup0, cw0, cb0, wdn0, gf, tm=512, seq_len=sp,
                                pages=(page_table[:half], ck_t, nblk))

    w_in = w_mlstm_in[0]
    w_in_bf = w_in[:, :2 * hk + 2 * hv].astype(BF16)
    wg_bf = jnp.pad(w_in[:, 2 * hk + 2 * hv:], ((0, 0), (0, LANES - 2 * M_HEADS))).astype(BF16)
    bg_col = b_mlstm_gate[0].reshape(2 * M_HEADS, 1)
    wo1 = w_mlstm_out[0].astype(BF16)
    gain = row(mlstm_head_gain[0])

    qk_p, vv_p, oo_p, gt_p = _mlstm_in(hp1, row(mlstm_norm[0]), w_in_bf, wg_bf, bg_col, tm=512, act_dtype=BF16)
    hc_p, c_p, n_p, m_p = _mlstm_seq(qk_p, vv_p, oo_p, gt_p, gain, bp, sp)

    g1, wup1, cw1, cb1, wdn1 = ffn_w(1)
    y_p, conv_p1, ksum_b = _ffn(hp1, hc_p.reshape(mp, d), wo1, g1, wup1, cw1, cb1, wdn1, gf, tm=512, seq_len=sp,
                                final=True, pages=(page_table[half:], ck_t, nblk))

    k_s, v_s, qt_s, kt_s, vt_s = _qkv_step(xs, row(attn_norm[0]), wqkv)
    sel = jnp.concatenate([_select(qt_s, ksum_a, nblk, 0), _select(qt_s, ksum_b, nblk, half)], axis=0)
    sel = sel[:, :, :MOBA_TOPK].reshape(bs, N_HEADS * MOBA_TOPK)
    attn_s_t = _decode_attn(sel, page_table, ck_t, cv_t, qt_s, kt_s, vt_s, bias_s)
    attn_s = attn_s_t.T[:bs].astype(BF16)
    hs1, conv_s0 = _ffn(xs, attn_s, wo0, g0, wup0, cw0, cb0, wdn0, gf, tm=bs, seq_len=1, state=state_conv[0])

    qk_s, vv_s, oo_s, gt_s = _mlstm_in(hs1, row(mlstm_norm[0]), w_in_bf, wg_bf, bg_col, tm=bs, act_dtype=F32)
    hc_s, c_s, n_s, m_s = _mlstm_step(qk_s, vv_s, oo_s, gt_s.T, gain, state_C[0], state_n[0], state_m[0])
    y_s, conv_s1 = _ffn(hs1, hc_s.reshape(bs, d), wo1, g1, wup1, cw1, cb1, wdn1, gf, tm=bs, seq_len=1,
                        state=state_conv[1], final=True)

    kv5 = lambda a, n: a.reshape(1, n, -1, N_HEADS, HEAD_DIM)
    kv5t = lambda a: jnp.transpose(a.reshape(1, bp, N_HEADS, HEAD_DIM, sp), (0, 1, 4, 2, 3))
    return (y_p.reshape(bp, sp, d), y_s.reshape(bs, ts, d),
            kv5t(kt_p), kv5t(vt_p), kv5(k_s, bs), kv5(v_s, bs),
            c_p[None], n_p.reshape(1, bp, M_HEADS, M_QK_DIM), m_p[:, :, 0, 0][None],
            c_s[None], n_s.reshape(1, bs, M_HEADS, M_QK_DIM), m_s.reshape(1, bs, M_HEADS),
            jnp.stack([conv_p0, conv_p1]), jnp.stack([conv_s0, conv_s1]))
```
